```python
import math
import jax
import jax.numpy as jnp
from jax import lax
import numpy as np

D_MODEL = 1024
BATCH = 4
SEQ = 8192
DEPTH = 2

RET_HEADS = 4
RET_QK_DIM = 128
RET_V_DIM = 128
RET_CHUNK = 128
ROPE_BASE = 10000.0
SSD_HEADS = 16
SSD_HEAD_DIM = 64
SSD_GROUPS = 2
SSD_STATE = 128
SSD_CONV = 4
SSD_CHUNK = 128
GLA_HEADS = 4
GLA_K_DIM = 64
GLA_V_DIM = 128
GLA_GATE_RANK = 16
GLA_TAU = 16.0
GLA_CHUNK = 16
D_FF = 4 * D_MODEL
EPS = 1e-6
N_BRANCHES = 3

RET_QK = RET_HEADS * RET_QK_DIM
RET_V = RET_HEADS * RET_V_DIM
SSD_INNER = SSD_HEADS * SSD_HEAD_DIM
SSD_CONV_DIM = SSD_INNER + 2 * SSD_GROUPS * SSD_STATE
GLA_QK = GLA_HEADS * GLA_K_DIM
GLA_V = GLA_HEADS * GLA_V_DIM
IN_SIZES = (RET_QK, RET_QK, RET_V, RET_V,
            SSD_INNER, SSD_CONV_DIM, SSD_HEADS,
            GLA_QK, GLA_QK, GLA_V, GLA_V, GLA_GATE_RANK,
            N_BRANCHES * D_MODEL)
IN_WIDTH = sum(IN_SIZES)

kernel_name = 'hybrid_retention_ssd_gla_gated_block'


def _split_points(sizes):
    pts, acc = [], 0
    for s in sizes[:-1]:
        acc += s
        pts.append(acc)
    return pts


def _rmsnorm(x, w):
    xf = x.astype(jnp.float32)
    y = xf * lax.rsqrt(jnp.mean(xf * xf, axis=-1, keepdims=True) + EPS)
    return (y * w.astype(jnp.float32)).astype(x.dtype)


def _group_rmsnorm(x, w, groups):
    shp = x.shape
    xf = x.astype(jnp.float32).reshape(shp[:-1] + (groups, shp[-1] // groups))
    y = xf * lax.rsqrt(jnp.mean(xf * xf, axis=-1, keepdims=True) + EPS)
    return (y.reshape(shp) * w.astype(jnp.float32)).astype(x.dtype)


def _rotary(x, cos, sin):
    x1, x2 = jnp.split(x, 2, axis=-1)
    c = cos[None, :, None, :]
    s = sin[None, :, None, :]
    return jnp.concatenate([x1 * c - x2 * s, x1 * s + x2 * c], axis=-1)


def _causal_conv(x, w, b):
    k, ch = w.shape
    y = lax.conv_general_dilated(x, w[:, None, :].astype(x.dtype), window_strides=(1,),
                                 padding=[(k - 1, 0)],
                                 dimension_numbers=('NWC', 'WIO', 'NWC'),
                                 feature_group_count=ch)
    return y + b


def _to_chunks(t, c):
    bsz, seq, h, d = t.shape
    return t.astype(jnp.float32).reshape(bsz, seq // c, c, h, d).transpose(0, 3, 1, 2, 4)


def _retention(q, k, v):
    out_dtype = q.dtype
    f32 = jnp.float32
    bsz, seq, h, dk = q.shape
    dv = v.shape[-1]
    c = RET_CHUNK
    qc = _to_chunks(q, c)
    kc = _to_chunks(k, c) * (dk ** -0.5)
    vc = _to_chunks(v, c)
    log_gamma = jnp.log1p(-jnp.exp2(-5.0 - jnp.arange(h, dtype=f32)))
    pos = jnp.arange(c, dtype=f32)
    dist = pos[:, None] - pos[None, :]
    causal = dist >= 0
    intra_decay = jnp.where(causal, jnp.exp(log_gamma[:, None, None] * jnp.where(causal, dist, 0.0)), 0.0)
    scores = jnp.einsum('bhnid,bhnjd->bhnij', qc, kc) * intra_decay[None, :, None]
    o_intra = jnp.einsum('bhnij,bhnje->bhnie', scores, vc)
    q_decay = jnp.exp(log_gamma[:, None] * (pos + 1.0))[None, :, :, None]
    k_decay = jnp.exp(log_gamma[:, None] * (c - 1.0 - pos))[None, :, :, None]
    chunk_decay = jnp.exp(log_gamma * c)[None, :, None, None]

    def step(state, inp):
        qn, kn, vn = inp
        o = jnp.einsum('bhcd,bhde->bhce', qn * q_decay, state)
        state = chunk_decay * state + jnp.einsum('bhcd,bhce->bhde', kn * k_decay, vn)
        return state, o

    init = jnp.zeros((bsz, h, dk, dv), f32)
    _, o_inter = lax.scan(step, init, (jnp.moveaxis(qc, 2, 0), jnp.moveaxis(kc, 2, 0), jnp.moveaxis(vc, 2, 0)))
    o = o_intra + jnp.moveaxis(o_inter, 0, 2)
    return o.transpose(0, 2, 3, 1, 4).reshape(bsz, seq, h, dv).astype(out_dtype)


def _ssd(x, dt, a_log, b_in, c_in, d_skip):
    out_dtype = x.dtype
    f32 = jnp.float32
    bsz, seq, n_heads, p = x.shape
    g, n_state = b_in.shape[2], b_in.shape[3]
    r = n_heads // g
    c = SSD_CHUNK
    nc = seq // c
    x = x.astype(f32)
    dt = dt.astype(f32)
    a = dt * -jnp.exp(a_log.astype(f32))
    xs = (x * dt[..., None]).reshape(bsz, nc, c, g, r, p)
    a = a.reshape(bsz, nc, c, g, r).transpose(0, 3, 4, 1, 2)
    bc = b_in.astype(f32).reshape(bsz, nc, c, g, n_state)
    cc = c_in.astype(f32).reshape(bsz, nc, c, g, n_state)
    a_cs = jnp.cumsum(a, axis=-1)
    causal = jnp.tril(jnp.ones((c, c), dtype=bool))
    l_mat = jnp.exp(jnp.where(causal, a_cs[..., :, None] - a_cs[..., None, :], -jnp.inf))
    cb = jnp.einsum('bnlgs,bnmgs->bgnlm', cc, bc)
    y_diag = jnp.einsum('bgnlm,bgrnlm,bnmgrp->bnlgrp', cb, l_mat, xs)
    decay_to_end = jnp.exp(a_cs[..., -1:] - a_cs)
    chunk_states = jnp.einsum('bnmgs,bgrnm,bnmgrp->nbgrps', bc, decay_to_end, xs)
    chunk_decay = jnp.moveaxis(jnp.exp(a_cs[..., -1]), -1, 0)

    def step(state, inp):
        st, dec = inp
        return dec[..., None, None] * state + st, state

    init = jnp.zeros((bsz, g, r, p, n_state), f32)
    _, prev_states = lax.scan(step, init, (chunk_states, chunk_decay))
    y_off = jnp.einsum('bnlgs,nbgrps,bgrnl->bnlgrp', cc, prev_states, jnp.exp(a_cs))
    y = (y_diag + y_off).reshape(bsz, seq, n_heads, p) + x * d_skip.astype(f32)[:, None]
    return y.astype(out_dtype)


def _gla(q, k, v, log_alpha):
    out_dtype = q.dtype
    f32 = jnp.float32
    bsz, seq, h, dk = q.shape
    dv = v.shape[-1]
    c = GLA_CHUNK
    qc = _to_chunks(q, c) * (dk ** -0.5)
    kc = _to_chunks(k, c)
    vc = _to_chunks(v, c)
    b = jnp.cumsum(_to_chunks(log_alpha, c), axis=3)
    b_last = b[:, :, :, -1:, :]
    q_in = qc * jnp.exp(b)
    k_in = kc * jnp.exp(-b)
    causal = jnp.tril(jnp.ones((c, c), dtype=bool))
    scores = jnp.where(causal, jnp.einsum('bhnid,bhnjd->bhnij', q_in, k_in), 0.0)
    o_intra = jnp.einsum('bhnij,bhnje->bhnie', scores, vc)
    k_state = kc * jnp.exp(b_last - b)
    chunk_decay = jnp.exp(b_last[:, :, :, 0, :])

    def step(state, inp):
        qn, kn, vn, dn = inp
        o = jnp.einsum('bhcd,bhde->bhce', qn, state)
        state = dn[..., None] * state + jnp.einsum('bhcd,bhce->bhde', kn, vn)
        return state, o

    init = jnp.zeros((bsz, h, dk, dv), f32)
    _, o_inter = lax.scan(step, init, (jnp.moveaxis(q_in, 2, 0), jnp.moveaxis(k_state, 2, 0),
                                       jnp.moveaxis(vc, 2, 0), jnp.moveaxis(chunk_decay, 2, 0)))
    o = o_intra + jnp.moveaxis(o_inter, 0, 2)
    return o.transpose(0, 2, 3, 1, 4).reshape(bsz, seq, h, dv).astype(out_dtype)


def _mixer_block(x, norm_w, w_in, ret_norm_w, ret_w_o, conv_w, conv_b, dt_bias, a_log, d_skip,
                 ssd_norm_w, ssd_w_o, gla_gate_w, gla_gate_b, gla_norm_w, gla_w_o,
                 merge_b, w_out, cos, sin):
    bsz, seq, _ = x.shape
    h = _rmsnorm(x, norm_w)
    proj = h @ w_in
    (rq, rk, rv, rg, sz, sxbc, sdt, gq, gk, gv, gr, glr, mg) = jnp.split(proj, _split_points(IN_SIZES), axis=-1)

    rq = _rotary(rq.reshape(bsz, seq, RET_HEADS, RET_QK_DIM), cos, sin)
    rk = _rotary(rk.reshape(bsz, seq, RET_HEADS, RET_QK_DIM), cos, sin)
    ro = _retention(rq, rk, rv.reshape(bsz, seq, RET_HEADS, RET_V_DIM))
    ro = _group_rmsnorm(ro.reshape(bsz, seq, RET_V), ret_norm_w, RET_HEADS)
    ret_out = (jax.nn.silu(rg) * ro) @ ret_w_o

    xbc = jax.nn.silu(_causal_conv(sxbc, conv_w, conv_b))
    sx, sb, sc = jnp.split(xbc, [SSD_INNER, SSD_INNER + SSD_GROUPS * SSD_STATE], axis=-1)
    dt = jax.nn.softplus(sdt + dt_bias)
    sy = _ssd(sx.reshape(bsz, seq, SSD_HEADS, SSD_HEAD_DIM), dt, a_log,
              sb.reshape(bsz, seq, SSD_GROUPS, SSD_STATE), sc.reshape(bsz, seq, SSD_GROUPS, SSD_STATE), d_skip)
    sy = _group_rmsnorm(sy.reshape(bsz, seq, SSD_INNER) * jax.nn.silu(sz), ssd_norm_w, SSD_GROUPS)
    ssd_out = sy @ ssd_w_o

    log_alpha = jax.nn.log_sigmoid((glr @ gla_gate_w + gla_gate_b).astype(jnp.float32)) / GLA_TAU
    go = _gla(gq.reshape(bsz, seq, GLA_HEADS, GLA_K_DIM), gk.reshape(bsz, seq, GLA_HEADS, GLA_K_DIM),
              gv.reshape(bsz, seq, GLA_HEADS, GLA_V_DIM), log_alpha.reshape(bsz, seq, GLA_HEADS, GLA_K_DIM))
    go = _group_rmsnorm(go.reshape(bsz, seq, GLA_V), gla_norm_w, GLA_HEADS)
    gla_out = (jax.nn.silu(gr) * go) @ gla_w_o

    g_ret, g_ssd, g_gla = jnp.split(jax.nn.sigmoid(mg + merge_b), N_BRANCHES, axis=-1)
    merged = g_ret * ret_out + g_ssd * ssd_out + g_gla * gla_out
    return x + merged @ w_out


def _mlp_block(x, norm_w, w_up, w_down):
    h = _rmsnorm(x, norm_w)
    return x + jnp.square(jax.nn.relu(h @ w_up)) @ w_down


def setup_inputs(seed: int = 0) -> dict:
    key = jax.random.key(seed)
    ks = jax.random.split(key, 22)
    f32 = jnp.float32

    def normal(k, shape, scale):
        return jax.random.normal(k, shape, f32) * scale

    def gain(k, shape):
        return 1.0 + 0.02 * jax.random.normal(k, shape, f32)

    dt0 = jnp.exp(jax.random.uniform(ks[7], (DEPTH, SSD_HEADS), f32, math.log(1e-3), math.log(1e-1)))
    dt_bias = dt0 + jnp.log(-jnp.expm1(-dt0))
    a_log = jnp.log(jax.random.uniform(ks[8], (DEPTH, SSD_HEADS), f32, 1.0, 16.0))
    return {
        'x': normal(ks[0], (BATCH, SEQ, D_MODEL), 1.0),
        'attn_norm_w': gain(ks[1], (DEPTH, D_MODEL)),
        'w_in': normal(ks[2], (DEPTH, D_MODEL, IN_WIDTH), D_MODEL ** -0.5),
        'ret_norm_w': gain(ks[3], (DEPTH, RET_V)),
        'ret_w_o': normal(ks[4], (DEPTH, RET_V, D_MODEL), RET_V ** -0.5),
        'ssd_conv_w': normal(ks[5], (DEPTH, SSD_CONV, SSD_CONV_DIM), SSD_CONV ** -0.5),
        'ssd_conv_b': normal(ks[6], (DEPTH, SSD_CONV_DIM), 0.02),
        'ssd_dt_bias': dt_bias,
        'ssd_a_log': a_log,
        'ssd_d': gain(ks[9], (DEPTH, SSD_HEADS)),
        'ssd_norm_w': gain(ks[10], (DEPTH, SSD_INNER)),
        'ssd_w_o': normal(ks[11], (DEPTH, SSD_INNER, D_MODEL), SSD_INNER ** -0.5),
        'gla_gate_w': normal(ks[12], (DEPTH, GLA_GATE_RANK, GLA_QK), GLA_GATE_RANK ** -0.5),
        'gla_gate_b': normal(ks[13], (DEPTH, GLA_QK), 0.02),
        'gla_norm_w': gain(ks[14], (DEPTH, GLA_V)),
        'gla_w_o': normal(ks[15], (DEPTH, GLA_V, D_MODEL), GLA_V ** -0.5),
        'merge_gate_b': normal(ks[16], (DEPTH, N_BRANCHES * D_MODEL), 0.02),
        'w_out': normal(ks[17], (DEPTH, D_MODEL, D_MODEL), D_MODEL ** -0.5),
        'mlp_norm_w': gain(ks[18], (DEPTH, D_MODEL)),
        'w_up': normal(ks[19], (DEPTH, D_MODEL, D_FF), D_MODEL ** -0.5),
        'w_down': normal(ks[20], (DEPTH, D_FF, D_MODEL), D_FF ** -0.5),
        'final_norm_w': gain(ks[21], (D_MODEL,)),
    }


def reference(x, attn_norm_w, w_in, ret_norm_w, ret_w_o, ssd_conv_w, ssd_conv_b, ssd_dt_bias,
              ssd_a_log, ssd_d, ssd_norm_w, ssd_w_o, gla_gate_w, gla_gate_b, gla_norm_w, gla_w_o,
              merge_gate_b, w_out, mlp_norm_w, w_up, w_down, final_norm_w):
    seq = x.shape[1]
    inv_freq = ROPE_BASE ** (-jnp.arange(0, RET_QK_DIM, 2, dtype=jnp.float32) / RET_QK_DIM)
    ang = jnp.arange(seq, dtype=jnp.float32)[:, None] * inv_freq[None, :]
    cos = jnp.cos(ang).astype(x.dtype)
    sin = jnp.sin(ang).astype(x.dtype)
    for layer in range(DEPTH):
        x = _mixer_block(x, attn_norm_w[layer], w_in[layer], ret_norm_w[layer], ret_w_o[layer],
                         ssd_conv_w[layer], ssd_conv_b[layer], ssd_dt_bias[layer], ssd_a_log[layer],
                         ssd_d[layer], ssd_norm_w[layer], ssd_w_o[layer], gla_gate_w[layer],
                         gla_gate_b[layer], gla_norm_w[layer], gla_w_o[layer], merge_gate_b[layer],
                         w_out[layer], cos, sin)
        x = _mlp_block(x, mlp_norm_w[layer], w_up[layer], w_down[layer])
    return _rmsnorm(x, final_norm_w)
```

```python
import functools
import math

import numpy as np
import jax
import jax.numpy as jnp
from jax import lax
from jax.experimental import pallas as pl
from jax.experimental.pallas import tpu as pltpu

F32 = jnp.float32
BF16 = jnp.bfloat16

D_MODEL = 1024
RET_HEADS = 4
RET_DK = 128
SSD_HEADS = 16
SSD_P = 64
SSD_GROUPS = 2
SSD_N = 128
SSD_INNER = SSD_HEADS * SSD_P
SSD_CONV = 4
SSD_CONV_DIM = SSD_INNER + 2 * SSD_GROUPS * SSD_N
GLA_HEADS = 4
GLA_DK = 64
GLA_DV = 128
GLA_RANK = 16
GLA_TAU = 16.0
GLA_SUB = 16
D_FF = 4 * D_MODEL
EPS = 1e-6
ROPE_BASE = 10000.0
CHUNK = 128
LANE = 128
CONV_HALO = 8
MIXER_TILE = 256
MLP_TILE = 256
FF_CHUNK = 1024
VMEM_LIMIT = 56 * 1024 * 1024


def _mm(a, b):
    return lax.dot_general(a, b, (((1,), (0,)), ((), ())), preferred_element_type=F32)


def _mm_nt(a, b):
    return lax.dot_general(a, b, (((1,), (1,)), ((), ())), preferred_element_type=F32)


def _mm_tn(a, b):
    return lax.dot_general(a, b, (((0,), (0,)), ((), ())), preferred_element_type=F32)


def _split3(v):
    hi = v.astype(BF16)
    r1 = v - hi.astype(F32)
    mid = r1.astype(BF16)
    lo = (r1 - mid.astype(F32)).astype(BF16)
    return hi, mid, lo


def _mm_sel(sel, v):
    hi, mid, lo = _split3(v)
    return _mm(sel, hi) + _mm(sel, mid) + _mm(sel, lo)


def _mm_sel_r(v, sel):
    hi = v.astype(BF16)
    lo = (v - hi.astype(F32)).astype(BF16)
    return _mm(hi, sel) + _mm(lo, sel)


def _rmsnorm(x, w):
    return x * lax.rsqrt(jnp.mean(x * x, axis=-1, keepdims=True) + EPS) * w


def _sigmoid(x):
    return 1.0 / (1.0 + jnp.exp(-x))


def _silu(x):
    return x * _sigmoid(x)


def _softplus(x):
    return jnp.maximum(x, 0.0) + jnp.log1p(jnp.exp(-jnp.abs(x)))


def _tril_bf16(n):
    r = lax.broadcasted_iota(jnp.int32, (n, n), 0)
    c = lax.broadcasted_iota(jnp.int32, (n, n), 1)
    return r >= c, (r >= c).astype(BF16)


def _ret_kernel(x_ref, nw_ref, w_ref, cos_ref, sin_ref, dec_ref, qd_ref, kd_ref, gnw_ref, wo_ref,
                o_ref, p_ref, g_ref, st_ref, *, tl, chunk_decay):
    @pl.when(pl.program_id(1) == 0)
    def _():
        st_ref[...] = jnp.zeros_like(st_ref)

    h = _rmsnorm(x_ref[0], nw_ref[...]).astype(BF16)
    p_ref[...] = _mm(h, w_ref[...])
    hw = RET_HEADS * RET_DK
    for c in range(tl // CHUNK):
        rows = slice(c * CHUNK, (c + 1) * CHUNK)
        cos = cos_ref[rows, :]
        sin = sin_ref[rows, :]
        for hd in range(RET_HEADS):
            cols = slice(hd * RET_DK, (hd + 1) * RET_DK)
            q = p_ref[rows, cols]
            k = p_ref[rows, hw + hd * RET_DK: hw + (hd + 1) * RET_DK]
            v = p_ref[rows, 2 * hw + hd * RET_DK: 2 * hw + (hd + 1) * RET_DK].astype(BF16)
            gate = p_ref[rows, 3 * hw + hd * RET_DK: 3 * hw + (hd + 1) * RET_DK]
            q = q * cos + pltpu.roll(q, RET_DK // 2, 1) * sin
            k = k * cos + pltpu.roll(k, RET_DK // 2, 1) * sin
            qb = q.astype(BF16)
            kb = k.astype(BF16)
            scores = _mm_nt(qb, kb) * dec_ref[hd]
            state = st_ref[hd]
            o = _mm(scores.astype(BF16), v) + _mm((q * qd_ref[:, cols]).astype(BF16), state.astype(BF16))
            st_ref[hd] = chunk_decay[hd] * state + _mm_tn((k * kd_ref[:, cols]).astype(BF16), v)
            o = o * lax.rsqrt(jnp.mean(o * o, axis=-1, keepdims=True) + EPS) * gnw_ref[:, cols]
            g_ref[rows, cols] = (_silu(gate) * o).astype(BF16)
    o_ref[0] = _mm(g_ref[...], wo_ref[...]).astype(o_ref.dtype)


def _retention_tables():
    lg = np.log1p(-np.exp2(-5.0 - np.arange(RET_HEADS, dtype=np.float64)))
    pos = np.arange(CHUNK, dtype=np.float64)
    scale = RET_DK ** -0.5
    dist = pos[:, None] - pos[None, :]
    dec = np.where(dist >= 0, np.exp(lg[:, None, None] * np.maximum(dist, 0.0)), 0.0) * scale
    qd = np.repeat(np.exp(lg[None, :] * (pos[:, None] + 1.0)), RET_DK, axis=1)
    kd = np.repeat(np.exp(lg[None, :] * (CHUNK - 1.0 - pos[:, None])), RET_DK, axis=1) * scale
    cd = tuple(float(v) for v in np.exp(lg * CHUNK))
    return (jnp.asarray(dec, F32), jnp.asarray(qd, F32), jnp.asarray(kd, F32), cd)


def _const_spec(shape):
    nd = len(shape)
    return pl.BlockSpec(shape, lambda b, t: (0,) * nd)


def _retention_call(x, norm_w, w_r, cos2, sin2, gn_w, w_o, tl):
    bsz, seq, d = x.shape
    dec, qd, kd, cd = _retention_tables()
    hw = RET_HEADS * RET_DK
    return pl.pallas_call(
        functools.partial(_ret_kernel, tl=tl, chunk_decay=cd),
        grid=(bsz, seq // tl),
        in_specs=[
            pl.BlockSpec((1, tl, d), lambda b, t: (b, t, 0)),
            _const_spec((1, d)),
            _const_spec((d, 4 * hw)),
            pl.BlockSpec((tl, RET_DK), lambda b, t: (t, 0)),
            pl.BlockSpec((tl, RET_DK), lambda b, t: (t, 0)),
            _const_spec((RET_HEADS, CHUNK, CHUNK)),
            _const_spec((CHUNK, hw)),
            _const_spec((CHUNK, hw)),
            _const_spec((1, hw)),
            _const_spec((hw, d)),
        ],
        out_specs=pl.BlockSpec((1, tl, d), lambda b, t: (b, t, 0)),
        out_shape=jax.ShapeDtypeStruct((bsz, seq, d), BF16),
        scratch_shapes=[
            pltpu.VMEM((tl, 4 * hw), F32),
            pltpu.VMEM((tl, hw), BF16),
            pltpu.VMEM((RET_HEADS, RET_DK, RET_DK), F32),
        ],
        compiler_params=pltpu.CompilerParams(
            dimension_semantics=("arbitrary", "arbitrary"), vmem_limit_bytes=VMEM_LIMIT),
        name="retention",
    )(x, norm_w, w_r, cos2, sin2, dec, qd, kd, gn_w, w_o)


def _ssd_kernel(x_ref, nw_ref, w_ref, cw_ref, cb_ref, dtb_ref, alog_ref, dsk_ref, ex_ref, gnw_ref, wo_ref,
                o_ref, p_ref, xb_ref, xc_ref, y_ref, st_ref, *, tl):
    @pl.when(pl.program_id(1) == 0)
    def _():
        st_ref[...] = jnp.zeros_like(st_ref)
        xb_ref[0:CONV_HALO, :] = jnp.zeros((CONV_HALO, SSD_CONV_DIM), F32)

    h = _rmsnorm(x_ref[0], nw_ref[...]).astype(BF16)
    p_ref[...] = _mm(h, w_ref[...])
    xb_ref[CONV_HALO:CONV_HALO + tl, :] = p_ref[:, SSD_INNER:SSD_INNER + SSD_CONV_DIM]
    acc = cb_ref[...]
    for k in range(SSD_CONV):
        off = CONV_HALO - (SSD_CONV - 1) + k
        acc = acc + cw_ref[k:k + 1, :] * xb_ref[off:off + tl, :]
    xc_ref[...] = _silu(acc)
    xb_ref[0:CONV_HALO, :] = xb_ref[tl:tl + CONV_HALO, :]

    dt = _softplus(p_ref[:, SSD_INNER + SSD_CONV_DIM:] + dtb_ref[...])
    a = dt * (-jnp.exp(alog_ref[...]))
    causal, tril = _tril_bf16(CHUNK)
    gw = SSD_INNER // SSD_GROUPS
    hpg = SSD_HEADS // SSD_GROUPS
    for c in range(tl // CHUNK):
        rows = slice(c * CHUNK, (c + 1) * CHUNK)
        a_cs = _mm_sel(tril, a[rows])
        a_cst = a_cs.T
        a_last = a_cs[CHUNK - 1:CHUNK, :]
        dte = _mm_sel_r(dt[rows], ex_ref[...])
        ecs = _mm_sel_r(jnp.exp(a_cs), ex_ref[...])
        dend = _mm_sel_r(jnp.exp(a_last - a_cs), ex_ref[...])
        sx = xc_ref[rows, 0:SSD_INNER]
        xs = sx * dte
        xs_b = xs.astype(BF16)
        xd_b = (xs * dend).astype(BF16)
        for g in range(SSD_GROUPS):
            gc = slice(g * gw, (g + 1) * gw)
            bmat = xc_ref[rows, SSD_INNER + g * SSD_N: SSD_INNER + (g + 1) * SSD_N].astype(BF16)
            cmat = xc_ref[rows, SSD_INNER + (SSD_GROUPS + g) * SSD_N:
                          SSD_INNER + (SSD_GROUPS + g + 1) * SSD_N].astype(BF16)
            cb = _mm_nt(cmat, bmat)
            pieces = []
            for r in range(hpg):
                hh = g * hpg + r
                seg = a_cs[:, hh:hh + 1] - a_cst[hh:hh + 1, :]
                lmat = jnp.exp(jnp.where(causal, seg, -jnp.inf))
                pieces.append(_mm((cb * lmat).astype(BF16), xs_b[:, hh * SSD_P:(hh + 1) * SSD_P]))
            y_diag = jnp.concatenate(pieces, axis=1)
            state = st_ref[:, gc]
            y = y_diag + _mm(cmat, state.astype(BF16)) * ecs[:, gc] + sx[:, gc] * dsk_ref[:, gc]
            st_ref[:, gc] = ecs[CHUNK - 1:CHUNK, gc] * state + _mm_tn(bmat, xd_b[:, gc])
            y = y * _silu(p_ref[rows, gc])
            y = y * lax.rsqrt(jnp.mean(y * y, axis=-1, keepdims=True) + EPS) * gnw_ref[:, gc]
            y_ref[rows, gc] = y.astype(BF16)
    o_ref[0] = _mm(y_ref[...], wo_ref[...]).astype(o_ref.dtype)


def _ssd_call(x, norm_w, w_s, conv_w, conv_b, dt_bias, a_log, d_skip, gn_w, w_o, tl):
    bsz, seq, d = x.shape
    wcols = SSD_INNER + SSD_CONV_DIM + LANE
    pad = LANE - SSD_HEADS
    dtb = jnp.pad(dt_bias, (0, pad)).reshape(1, LANE)
    alog = jnp.pad(a_log, (0, pad)).reshape(1, LANE)
    dsk = jnp.repeat(d_skip, SSD_P).reshape(1, SSD_INNER)
    expand = np.zeros((LANE, SSD_INNER), np.float32)
    for hh in range(SSD_HEADS):
        expand[hh, hh * SSD_P:(hh + 1) * SSD_P] = 1.0
    expand = jnp.asarray(expand, BF16)
    return pl.pallas_call(
        functools.partial(_ssd_kernel, tl=tl),
        grid=(bsz, seq // tl),
        in_specs=[
            pl.BlockSpec((1, tl, d), lambda b, t: (b, t, 0)),
            _const_spec((1, d)),
            _const_spec((d, wcols)),
            _const_spec((SSD_CONV, SSD_CONV_DIM)),
            _const_spec((1, SSD_CONV_DIM)),
            _const_spec((1, LANE)),
            _const_spec((1, LANE)),
            _const_spec((1, SSD_INNER)),
            _const_spec((LANE, SSD_INNER)),
            _const_spec((1, SSD_INNER)),
            _const_spec((SSD_INNER, d)),
        ],
        out_specs=pl.BlockSpec((1, tl, d), lambda b, t: (b, t, 0)),
        out_shape=jax.ShapeDtypeStruct((bsz, seq, d), BF16),
        scratch_shapes=[
            pltpu.VMEM((tl, wcols), F32),
            pltpu.VMEM((tl + CONV_HALO, SSD_CONV_DIM), F32),
            pltpu.VMEM((tl, SSD_CONV_DIM), F32),
            pltpu.VMEM((tl, SSD_INNER), BF16),
            pltpu.VMEM((SSD_N, SSD_INNER), F32),
        ],
        compiler_params=pltpu.CompilerParams(
            dimension_semantics=("arbitrary", "arbitrary"), vmem_limit_bytes=VMEM_LIMIT),
        name="ssd",
    )(x, norm_w, w_s, conv_w, conv_b.reshape(1, -1), dtb, alog, dsk, expand, gn_w, w_o)


def _gla_kernel(x_ref, nw_ref, w_ref, gw_ref, gb_ref, gnw_ref, wo_ref,
                o_ref, p_ref, g_ref, st_ref, *, tl):
    @pl.when(pl.program_id(1) == 0)
    def _():
        st_ref[...] = jnp.zeros_like(st_ref)

    qk = GLA_HEADS * GLA_DK
    vw = GLA_HEADS * GLA_DV
    h = _rmsnorm(x_ref[0], nw_ref[...]).astype(BF16)
    p_ref[...] = _mm(h, w_ref[...])
    z = _mm(p_ref[:, 2 * qk + 2 * vw:].astype(BF16), gw_ref[...]) + gb_ref[...]
    la = -_softplus(-z) * (1.0 / GLA_TAU)
    causal, tril = _tril_bf16(CHUNK)
    row = lax.broadcasted_iota(jnp.int32, (CHUNK, qk), 0)
    lane = lax.broadcasted_iota(jnp.int32, (CHUNK, LANE), 1)
    head_mask = (lane < GLA_DK, lane >= GLA_DK)
    nsub = CHUNK // GLA_SUB
    scale = GLA_DK ** -0.5
    for c in range(tl // CHUNK):
        rows = slice(c * CHUNK, (c + 1) * CHUNK)
        bcs = _mm_sel(tril, la[rows])
        eprev = [jnp.zeros((1, qk), F32)] + [bcs[GLA_SUB * j - 1:GLA_SUB * j, :] for j in range(1, nsub)]
        e_last = bcs[CHUNK - 1:CHUNK, :]
        eprev_x = jnp.concatenate([jnp.broadcast_to(e, (GLA_SUB, qk)) for e in eprev], axis=0)
        k_c = p_ref[rows, qk:2 * qk]
        q_in = p_ref[rows, 0:qk] * scale * jnp.exp(bcs - eprev_x)
        q_in_b = q_in.astype(BF16)
        qb = (q_in * jnp.exp(eprev_x)).astype(BF16)
        kend = (k_c * jnp.exp(e_last - bcs)).astype(BF16)
        zero_b = jnp.zeros((CHUNK, LANE), BF16)
        qm, qbm, kem = [], [], []
        for hd in range(GLA_HEADS):
            lanes = slice((hd // 2) * LANE, (hd // 2 + 1) * LANE)
            m = head_mask[hd % 2]
            qm.append(jnp.where(m, q_in_b[:, lanes], zero_b))
            qbm.append(jnp.where(m, qb[:, lanes], zero_b))
            kem.append(jnp.where(m, kend[:, lanes], zero_b))
        srows = [[] for _ in range(GLA_HEADS)]
        for j in range(nsub):
            arg = jnp.where(row < GLA_SUB * (j + 1), eprev[j] - bcs, -jnp.inf)
            khat = (k_c * jnp.exp(arg)).astype(BF16)
            for hd in range(GLA_HEADS):
                lanes = slice((hd // 2) * LANE, (hd // 2 + 1) * LANE)
                srows[hd].append(_mm_nt(qm[hd][GLA_SUB * j:GLA_SUB * (j + 1), :], khat[:, lanes]))
        upd = []
        for hd in range(GLA_HEADS):
            lanes = slice((hd // 2) * LANE, (hd // 2 + 1) * LANE)
            cols = slice(hd * GLA_DV, (hd + 1) * GLA_DV)
            scores = jnp.where(causal, jnp.concatenate(srows[hd], axis=0), 0.0).astype(BF16)
            v = p_ref[rows, 2 * qk + hd * GLA_DV: 2 * qk + (hd + 1) * GLA_DV].astype(BF16)
            o = _mm(scores, v) + _mm_nt(qbm[hd], st_ref[:, lanes].astype(BF16))
            upd.append(_mm_tn(v, kem[hd]))
            o = o * lax.rsqrt(jnp.mean(o * o, axis=-1, keepdims=True) + EPS) * gnw_ref[:, cols]
            gate = p_ref[rows, 2 * qk + vw + hd * GLA_DV: 2 * qk + vw + (hd + 1) * GLA_DV]
            g_ref[rows, cols] = (_silu(gate) * o).astype(BF16)
        for pr in range(GLA_HEADS // 2):
            lanes = slice(pr * LANE, (pr + 1) * LANE)
            st_ref[:, lanes] = jnp.exp(e_last[:, lanes]) * st_ref[:, lanes] + upd[2 * pr] + upd[2 * pr + 1]
    o_ref[0] = _mm(g_ref[...], wo_ref[...]).astype(o_ref.dtype)


def _gla_call(x, norm_w, w_g, gate_w, gate_b, gn_w, w_o, tl):
    bsz, seq, d = x.shape
    qk = GLA_HEADS * GLA_DK
    vw = GLA_HEADS * GLA_DV
    wcols = 2 * qk + 2 * vw + LANE
    gw = jnp.pad(gate_w, ((0, LANE - GLA_RANK), (0, 0))).astype(BF16)
    return pl.pallas_call(
        functools.partial(_gla_kernel, tl=tl),
        grid=(bsz, seq // tl),
        in_specs=[
            pl.BlockSpec((1, tl, d), lambda b, t: (b, t, 0)),
            _const_spec((1, d)),
            _const_spec((d, wcols)),
            _const_spec((LANE, qk)),
            _const_spec((1, qk)),
            _const_spec((1, vw)),
            _const_spec((vw, d)),
        ],
        out_specs=pl.BlockSpec((1, tl, d), lambda b, t: (b, t, 0)),
        out_shape=jax.ShapeDtypeStruct((bsz, seq, d), BF16),
        scratch_shapes=[
            pltpu.VMEM((tl, wcols), F32),
            pltpu.VMEM((tl, vw), BF16),
            pltpu.VMEM((GLA_DV, qk), F32),
        ],
        compiler_params=pltpu.CompilerParams(
            dimension_semantics=("arbitrary", "arbitrary"), vmem_limit_bytes=VMEM_LIMIT),
        name="gla",
    )(x, norm_w, w_g, gw, gate_b.reshape(1, -1), gn_w, w_o)


def _merge_mlp_kernel(x_ref, nw_ref, wg_ref, gb_ref, a_ref, b_ref, c_ref, wout_ref, mnw_ref, wup_ref, wdn_ref,
                      fnw_ref, o_ref, *, final_norm):
    x = x_ref[0]
    d = x.shape[-1]
    h = _rmsnorm(x, nw_ref[...]).astype(BF16)
    merged = None
    for i, br in enumerate((a_ref, b_ref, c_ref)):
        gate = _sigmoid(_mm(h, wg_ref[:, i * d:(i + 1) * d]) + gb_ref[:, i * d:(i + 1) * d])
        term = gate * br[0].astype(F32)
        merged = term if merged is None else merged + term
    x1 = x + _mm(merged.astype(BF16), wout_ref[...])
    h2 = _rmsnorm(x1, mnw_ref[...]).astype(BF16)
    acc = x1
    for k in range(D_FF // FF_CHUNK):
        u = jnp.maximum(_mm(h2, wup_ref[:, k * FF_CHUNK:(k + 1) * FF_CHUNK]), 0.0)
        acc = acc + _mm((u * u).astype(BF16), wdn_ref[k * FF_CHUNK:(k + 1) * FF_CHUNK, :])
    if final_norm:
        acc = _rmsnorm(acc, fnw_ref[...])
    o_ref[0] = acc


def _merge_mlp_call(x, norm_w, w_mg, merge_b, ret_o, ssd_o, gla_o, w_out, mlp_norm_w, w_up, w_down,
                    final_norm_w, final_norm, tl):
    bsz, seq, d = x.shape
    tok = pl.BlockSpec((1, tl, d), lambda b, t: (b, t, 0))
    return pl.pallas_call(
        functools.partial(_merge_mlp_kernel, final_norm=final_norm),
        grid=(bsz, seq // tl),
        in_specs=[
            tok,
            _const_spec((1, d)),
            _const_spec((d, 3 * d)),
            _const_spec((1, 3 * d)),
            tok, tok, tok,
            _const_spec((d, d)),
            _const_spec((1, d)),
            _const_spec((d, D_FF)),
            _const_spec((D_FF, d)),
            _const_spec((1, d)),
        ],
        out_specs=tok,
        out_shape=jax.ShapeDtypeStruct((bsz, seq, d), F32),
        compiler_params=pltpu.CompilerParams(
            dimension_semantics=("arbitrary", "arbitrary"), vmem_limit_bytes=VMEM_LIMIT),
        name="merge_mlp",
    )(x, norm_w, w_mg, merge_b, ret_o, ssd_o, gla_o, w_out, mlp_norm_w, w_up, w_down, final_norm_w)


def _pad_cols(w, width):
    return jnp.pad(w, ((0, 0), (0, width - w.shape[1])))


def kernel(x, attn_norm_w, w_in, ret_norm_w, ret_w_o, ssd_conv_w, ssd_conv_b, ssd_dt_bias, ssd_a_log, ssd_d,
           ssd_norm_w, ssd_w_o, gla_gate_w, gla_gate_b, gla_norm_w, gla_w_o, merge_gate_b, w_out, mlp_norm_w,
           w_up, w_down, final_norm_w):
    depth = w_in.shape[0]
    seq = x.shape[1]
    d = x.shape[2]
    tl = min(MIXER_TILE, seq)
    tm = min(MLP_TILE, seq)
    inv_freq = ROPE_BASE ** (-jnp.arange(0, RET_DK, 2, dtype=F32) / RET_DK)
    ang = jnp.arange(seq, dtype=F32)[:, None] * inv_freq[None, :]
    cos = jnp.cos(ang)
    sin = jnp.sin(ang)
    cos2 = jnp.concatenate([cos, cos], axis=1)
    sin2 = jnp.concatenate([-sin, sin], axis=1)

    o_ret = 0
    o_sz = 4 * RET_HEADS * RET_DK
    o_sdt = o_sz + SSD_INNER + SSD_CONV_DIM
    o_gla = o_sdt + SSD_HEADS
    o_glr = o_gla + 2 * GLA_HEADS * GLA_DK + 2 * GLA_HEADS * GLA_DV
    o_mg = o_glr + GLA_RANK
    row = lambda v: v.reshape(1, -1)
    for layer in range(depth):
        wl = w_in[layer]
        w_r = wl[:, o_ret:o_sz].astype(BF16)
        w_s = jnp.concatenate([wl[:, o_sz:o_sdt], _pad_cols(wl[:, o_sdt:o_gla], LANE)], axis=1).astype(BF16)
        w_g = jnp.concatenate([wl[:, o_gla:o_glr], _pad_cols(wl[:, o_glr:o_mg], LANE)], axis=1).astype(BF16)
        w_m = wl[:, o_mg:].astype(BF16)
        nw = row(attn_norm_w[layer])
        ret_o = _retention_call(x, nw, w_r, cos2, sin2, row(ret_norm_w[layer]),
                                ret_w_o[layer].astype(BF16), tl)
        ssd_o = _ssd_call(x, nw, w_s, ssd_conv_w[layer], ssd_conv_b[layer], ssd_dt_bias[layer],
                          ssd_a_log[layer], ssd_d[layer], row(ssd_norm_w[layer]),
                          ssd_w_o[layer].astype(BF16), tl)
        gla_o = _gla_call(x, nw, w_g, gla_gate_w[layer], gla_gate_b[layer], row(gla_norm_w[layer]),
                          gla_w_o[layer].astype(BF16), tl)
        x = _merge_mlp_call(x, nw, w_m, row(merge_gate_b[layer]), ret_o, ssd_o, gla_o,
                            w_out[layer].astype(BF16), row(mlp_norm_w[layer]), w_up[layer].astype(BF16),
                            w_down[layer].astype(BF16), row(final_norm_w), layer == depth - 1, tm)
    return x
```

```python
import functools
import math

import numpy as np
import jax
import jax.numpy as jnp
from jax import lax
from jax.experimental import pallas as pl
from jax.experimental.pallas import tpu as pltpu

F32 = jnp.float32
BF16 = jnp.bfloat16

D_MODEL = 1024
RET_HEADS = 4
RET_DK = 128
SSD_HEADS = 16
SSD_P = 64
SSD_GROUPS = 2
SSD_N = 128
SSD_INNER = SSD_HEADS * SSD_P
SSD_CONV = 4
SSD_CONV_DIM = SSD_INNER + 2 * SSD_GROUPS * SSD_N
GLA_HEADS = 4
GLA_DK = 64
GLA_DV = 128
GLA_RANK = 16
GLA_TAU = 16.0
GLA_SUB = 16
D_FF = 4 * D_MODEL
EPS = 1e-6
ROPE_BASE = 10000.0
LOG2E = math.log2(math.e)
CHUNK = 128
LANE = 128
SUBLANE = 8
SLAB = 256
SSD_REP = 3
MIXER_TILE = 256
MLP_TILE = 512
FF_CHUNK = 1024
VMEM_LIMIT = 56 * 1024 * 1024


def _mm(a, b):
    return lax.dot_general(a, b, (((1,), (0,)), ((), ())), preferred_element_type=F32)


def _split3(v):
    hi = v.astype(BF16)
    r1 = v - hi.astype(F32)
    mid = r1.astype(BF16)
    lo = (r1 - mid.astype(F32)).astype(BF16)
    return hi, mid, lo


def _mm_sel(sel, v):
    hi, mid, lo = _split3(v)
    return _mm(sel, hi) + _mm(sel, mid) + _mm(sel, lo)


def _rmsnorm(x, w):
    return x * lax.rsqrt(jnp.mean(x * x, axis=-1, keepdims=True) + EPS) * w


def _sigmoid(x):
    return 0.5 + 0.5 * jnp.tanh(0.5 * x)


def _silu(x):
    h = 0.5 * x
    return h + h * jnp.tanh(h)


def _softplus(x):
    return jnp.maximum(x, 0.0) + jnp.log(1.0 + jnp.exp(-jnp.abs(x)))


def _tril_bf16(n):
    r = lax.broadcasted_iota(jnp.int32, (n, n), 0)
    c = lax.broadcasted_iota(jnp.int32, (n, n), 1)
    return r >= c, (r >= c).astype(BF16)


def _const_spec(shape):
    nd = len(shape)
    return pl.BlockSpec(shape, lambda s: (0,) * nd)


def _skewed_specs(bsz, seq, tl, d):
    nt = seq // tl
    assert seq % tl == 0 and nt % 2 == 0
    total = bsz * nt
    tile = lambda g: (g // nt, g % nt, 0)
    x_first = pl.BlockSpec((1, tl, d), lambda s: (0, 0, 0))
    x_odd = pl.BlockSpec((1, tl, d), lambda s: tile(2 * s + 1))
    x_next = pl.BlockSpec((1, tl, d), lambda s: tile(jnp.minimum(2 * s + 2, total - 1)))
    out = pl.BlockSpec((1, 2 * tl, d), lambda s: ((2 * s) // nt, ((2 * s) % nt) // 2, 0))
    return nt, total // 2, (x_first, x_odd, x_next), out


def _mixer_params():
    return pltpu.CompilerParams(dimension_semantics=("arbitrary",), vmem_limit_bytes=VMEM_LIMIT)


def _ret_kernel(x0_ref, xa_ref, xb_ref, nw_ref, w_ref, cos_ref, sin_ref, dec_ref, qd_ref, kd_ref, gnw_ref,
                wo_ref, o_ref, pa_ref, pb_ref, g_ref, st_ref, *, tl, nt, chunk_decay):
    s = pl.program_id(0)
    hw = RET_HEADS * RET_DK

    def inproj(x_ref, p_ref):
        p_ref[...] = _mm(_rmsnorm(x_ref[0], nw_ref[...]).astype(BF16), w_ref[...])

    def step(p_ref, half, xn_ref, pn_ref):
        base = half * tl
        hn = _rmsnorm(xn_ref[0], nw_ref[...]).astype(BF16)
        units = [(c, hd) for c in range(tl // CHUNK) for hd in range(RET_HEADS)]
        slab = 4 * hw // len(units)
        for u, (c, hd) in enumerate(units):
            rows = slice(c * CHUNK, (c + 1) * CHUNK)
            orow = slice(base + c * CHUNK, base + (c + 1) * CHUNK)
            cols = slice(hd * RET_DK, (hd + 1) * RET_DK)
            cos = cos_ref[orow, :]
            sin = sin_ref[orow, :]
            q = p_ref[rows, cols]
            k = p_ref[rows, hw + hd * RET_DK: hw + (hd + 1) * RET_DK]
            v = p_ref[rows, 2 * hw + hd * RET_DK: 2 * hw + (hd + 1) * RET_DK].astype(BF16)
            q = q * cos + pltpu.roll(q, RET_DK // 2, 1) * sin
            k = k * cos + pltpu.roll(k, RET_DK // 2, 1) * sin
            k_t = k.T
            scores = _mm(q.astype(BF16), k_t.astype(BF16))
            kv = _mm((k_t * kd_ref[cols, :]).astype(BF16), v)
            pn_ref[:, u * slab:(u + 1) * slab] = _mm(hn, w_ref[:, u * slab:(u + 1) * slab])
            state = st_ref[hd]
            o = (_mm((scores * dec_ref[hd]).astype(BF16), v)
                 + _mm((q * qd_ref[:, cols]).astype(BF16), state.astype(BF16)))
            st_ref[hd] = chunk_decay[hd] * state + kv
            o = o * lax.rsqrt(jnp.mean(o * o, axis=-1, keepdims=True) + EPS) * gnw_ref[:, cols]
            gate = p_ref[rows, 3 * hw + hd * RET_DK: 3 * hw + (hd + 1) * RET_DK]
            g_ref[orow, cols] = (_silu(gate) * o).astype(BF16)
        o_ref[0, base:base + tl, :] = _mm(g_ref[base:base + tl, :], wo_ref[...]).astype(o_ref.dtype)

    @pl.when(s == 0)
    def _():
        inproj(x0_ref, pa_ref)

    @pl.when((2 * s) % nt == 0)
    def _():
        st_ref[...] = jnp.zeros_like(st_ref)

    step(pa_ref, 0, xa_ref, pb_ref)
    step(pb_ref, 1, xb_ref, pa_ref)


def _retention_tables():
    lg = np.log1p(-np.exp2(-5.0 - np.arange(RET_HEADS, dtype=np.float64)))
    pos = np.arange(CHUNK, dtype=np.float64)
    scale = RET_DK ** -0.5
    dist = pos[:, None] - pos[None, :]
    dec = np.where(dist >= 0, np.exp(lg[:, None, None] * np.maximum(dist, 0.0)), 0.0) * scale
    qd = np.repeat(np.exp(lg[None, :] * (pos[:, None] + 1.0)), RET_DK, axis=1)
    kd = np.repeat(np.exp(lg[:, None] * (CHUNK - 1.0 - pos[None, :])), RET_DK, axis=0) * scale
    cd = tuple(float(v) for v in np.exp(lg * CHUNK))
    return (jnp.asarray(dec, F32), jnp.asarray(qd, F32), jnp.asarray(kd, F32), cd)


def _retention_call(x, norm_w, w_r, cos2, sin2, gn_w, w_o, tl):
    bsz, seq, d = x.shape
    dec, qd, kd, cd = _retention_tables()
    hw = RET_HEADS * RET_DK
    nt, steps, x_specs, out_spec = _skewed_specs(bsz, seq, tl, d)
    rope = pl.BlockSpec((2 * tl, RET_DK), lambda s: (((2 * s) % nt) // 2, 0))
    return pl.pallas_call(
        functools.partial(_ret_kernel, tl=tl, nt=nt, chunk_decay=cd),
        grid=(steps,),
        in_specs=[
            *x_specs,
            _const_spec((1, d)),
            _const_spec((d, 4 * hw)),
            rope, rope,
            _const_spec((RET_HEADS, CHUNK, CHUNK)),
            _const_spec((CHUNK, hw)),
            _const_spec((hw, CHUNK)),
            _const_spec((1, hw)),
            _const_spec((hw, d)),
        ],
        out_specs=out_spec,
        out_shape=jax.ShapeDtypeStruct((bsz, seq, d), BF16),
        scratch_shapes=[
            pltpu.VMEM((tl, 4 * hw), F32),
            pltpu.VMEM((tl, 4 * hw), F32),
            pltpu.VMEM((2 * tl, hw), BF16),
            pltpu.VMEM((RET_HEADS, RET_DK, RET_DK), F32),
        ],
        compiler_params=_mixer_params(),
        name="retention",
    )(x, x, x, norm_w, w_r, cos2, sin2, dec, qd, kd, gn_w, w_o)


def _ssd_kernel(x0_ref, xa_ref, xb_ref, nw_ref, w_ref, cw_ref, cb_ref, dtb_ref, alog_ref, dsk_ref, ex_ref,
                gnw_ref, wo_ref, o_ref, pa_ref, pb_ref, hx_ref, hb_ref, xc_ref, y_ref, st_ref, *, tl, nt):
    s = pl.program_id(0)
    gw = SSD_INNER // SSD_GROUPS
    hpg = SSD_HEADS // SSD_GROUPS
    xbc = slice(SSD_INNER, SSD_INNER + SSD_CONV_DIM)

    def inproj(x_ref, p_ref):
        p_ref[...] = _mm(_rmsnorm(x_ref[0], nw_ref[...]).astype(BF16), w_ref[...])

    def shift_rows(v, n, hist):
        r = pltpu.roll(v, n, 0)
        sub = lax.broadcasted_iota(jnp.int32, (SUBLANE, v.shape[1]), 0)
        head = jnp.where(sub < n, pltpu.roll(hist, n, 0), r[0:SUBLANE])
        return jnp.concatenate([head, r[SUBLANE:]], axis=0)

    def step(p_ref, half, xn_ref, pn_ref):
        base = half * tl
        hn = _rmsnorm(xn_ref[0], nw_ref[...]).astype(BF16)
        wcols = w_ref.shape[1]
        slabs = [(c0, min(c0 + SLAB, wcols)) for c0 in range(0, wcols, SLAB)]

        def emit(n):
            for _ in range(min(n, len(slabs))):
                c0, c1 = slabs.pop(0)
                pn_ref[:, c0:c1] = _mm(hn, w_ref[:, c0:c1])

        lane = lax.broadcasted_iota(jnp.int32, (CHUNK, LANE), 1)

        def expand(v):
            hi, mid, lo = _split3(v)
            packed = jnp.where(lane < SSD_HEADS, hi, jnp.where(lane < 2 * SSD_HEADS, mid, lo))
            return _mm(packed, ex_ref[...])

        xin = p_ref[:, xbc]
        x1 = shift_rows(xin, 1, hx_ref[...])
        hx_ref[...] = xin[tl - SUBLANE:, :]
        near = cb_ref[...] + cw_ref[3:4, :] * xin + cw_ref[2:3, :] * x1
        far = cw_ref[1:2, :] * xin + cw_ref[0:1, :] * x1
        xc_ref[...] = _silu(near + shift_rows(far, 2, hb_ref[...]))
        hb_ref[...] = far[tl - SUBLANE:, :]
        emit(3)

        dt = _softplus(p_ref[:, SSD_INNER + SSD_CONV_DIM:] + dtb_ref[...])
        a = dt * (-jnp.exp(alog_ref[...]))
        causal, tril = _tril_bf16(CHUNK)
        zero_b = jnp.zeros((CHUNK, LANE), BF16)
        nchunk = tl // CHUNK
        for c in range(nchunk):
            rows = slice(c * CHUNK, (c + 1) * CHUNK)
            orow = slice(base + c * CHUNK, base + (c + 1) * CHUNK)
            a_cs = _mm_sel(tril, a[rows])
            a_last = a_cs[CHUNK - 1:CHUNK, :]
            a2 = a_cs * LOG2E
            a2t = a2.T
            dte = expand(dt[rows])
            ecs = expand(jnp.exp(a_cs))
            dend = expand(jnp.exp(a_last - a_cs))
            emit(1)
            sx = xc_ref[rows, 0:SSD_INNER]
            xs = sx * dte
            xs_b = xs.astype(BF16)
            xd_b = (xs * dend).astype(BF16)
            for g in range(SSD_GROUPS):
                gc = slice(g * gw, (g + 1) * gw)
                bmat_t = xc_ref[rows, SSD_INNER + g * SSD_N: SSD_INNER + (g + 1) * SSD_N].T.astype(BF16)
                cmat = xc_ref[rows, SSD_INNER + (SSD_GROUPS + g) * SSD_N:
                              SSD_INNER + (SSD_GROUPS + g + 1) * SSD_N].astype(BF16)
                cb = jnp.where(causal, _mm(cmat, bmat_t), 0.0)
                pieces = []
                for r in range(0, hpg, 2):
                    hh = g * hpg + r
                    pair = xs_b[:, hh * SSD_P:(hh + 2) * SSD_P]
                    rhs = jnp.concatenate([jnp.where(lane < SSD_P, pair, zero_b),
                                           jnp.where(lane >= SSD_P, pair, zero_b)], axis=0)
                    lhs = []
                    for h2 in (hh, hh + 1):
                        seg = jnp.minimum(a2[:, h2:h2 + 1] - a2t[h2:h2 + 1, :], 0.0)
                        lhs.append((cb * jnp.exp2(seg)).astype(BF16))
                    pieces.append(_mm(jnp.concatenate(lhs, axis=1), rhs))
                y_diag = jnp.concatenate(pieces, axis=1)
                state = st_ref[:, gc]
                y = y_diag + _mm(cmat, state.astype(BF16)) * ecs[:, gc] + sx[:, gc] * dsk_ref[:, gc]
                st_ref[:, gc] = ecs[CHUNK - 1:CHUNK, gc] * state + _mm(bmat_t, xd_b[:, gc])
                y = y * _silu(p_ref[rows, gc])
                y = y * lax.rsqrt(jnp.mean(y * y, axis=-1, keepdims=True) + EPS) * gnw_ref[:, gc]
                y_ref[orow, gc] = y.astype(BF16)
                emit(2 if c < nchunk - 1 or g < SSD_GROUPS - 1 else len(slabs))
        o_ref[0, base:base + tl, :] = _mm(y_ref[base:base + tl, :], wo_ref[...]).astype(o_ref.dtype)

    @pl.when(s == 0)
    def _():
        inproj(x0_ref, pa_ref)

    @pl.when((2 * s) % nt == 0)
    def _():
        st_ref[...] = jnp.zeros_like(st_ref)
        hx_ref[...] = jnp.zeros_like(hx_ref)
        hb_ref[...] = jnp.zeros_like(hb_ref)

    step(pa_ref, 0, xa_ref, pb_ref)
    step(pb_ref, 1, xb_ref, pa_ref)


def _ssd_call(x, norm_w, w_s, conv_w, conv_b, dt_bias, a_log, d_skip, gn_w, w_o, tl):
    bsz, seq, d = x.shape
    wcols = SSD_INNER + SSD_CONV_DIM + LANE
    pad = LANE - SSD_REP * SSD_HEADS
    dtb = jnp.pad(jnp.tile(dt_bias, SSD_REP), (0, pad)).reshape(1, LANE)
    alog = jnp.pad(jnp.tile(a_log, SSD_REP), (0, pad)).reshape(1, LANE)
    dsk = jnp.repeat(d_skip, SSD_P).reshape(1, SSD_INNER)
    expand = np.zeros((LANE, SSD_INNER), np.float32)
    for r in range(SSD_REP * SSD_HEADS):
        hh = r % SSD_HEADS
        expand[r, hh * SSD_P:(hh + 1) * SSD_P] = 1.0
    expand = jnp.asarray(expand, BF16)
    nt, steps, x_specs, out_spec = _skewed_specs(bsz, seq, tl, d)
    return pl.pallas_call(
        functools.partial(_ssd_kernel, tl=tl, nt=nt),
        grid=(steps,),
        in_specs=[
            *x_specs,
            _const_spec((1, d)),
            _const_spec((d, wcols)),
            _const_spec((SSD_CONV, SSD_CONV_DIM)),
            _const_spec((1, SSD_CONV_DIM)),
            _const_spec((1, LANE)),
            _const_spec((1, LANE)),
            _const_spec((1, SSD_INNER)),
            _const_spec((LANE, SSD_INNER)),
            _const_spec((1, SSD_INNER)),
            _const_spec((SSD_INNER, d)),
        ],
        out_specs=out_spec,
        out_shape=jax.ShapeDtypeStruct((bsz, seq, d), BF16),
        scratch_shapes=[
            pltpu.VMEM((tl, wcols), F32),
            pltpu.VMEM((tl, wcols), F32),
            pltpu.VMEM((SUBLANE, SSD_CONV_DIM), F32),
            pltpu.VMEM((SUBLANE, SSD_CONV_DIM), F32),
            pltpu.VMEM((tl, SSD_CONV_DIM), F32),
            pltpu.VMEM((2 * tl, SSD_INNER), BF16),
            pltpu.VMEM((SSD_N, SSD_INNER), F32),
        ],
        compiler_params=_mixer_params(),
        name="ssd",
    )(x, x, x, norm_w, w_s, conv_w, conv_b.reshape(1, -1), dtb, alog, dsk, expand, gn_w, w_o)


def _gla_kernel(x0_ref, xa_ref, xb_ref, nw_ref, w_ref, gw_ref, gb_ref, gnw_ref, wo_ref,
                o_ref, pa_ref, pb_ref, g_ref, st_ref, *, tl, nt):
    s = pl.program_id(0)
    qk = GLA_HEADS * GLA_DK
    vw = GLA_HEADS * GLA_DV
    nsub = CHUNK // GLA_SUB
    scale = GLA_DK ** -0.5

    def inproj(x_ref, p_ref):
        p_ref[...] = _mm(_rmsnorm(x_ref[0], nw_ref[...]).astype(BF16), w_ref[...])

    def step(p_ref, half, xn_ref, pn_ref):
        base = half * tl
        hn = _rmsnorm(xn_ref[0], nw_ref[...]).astype(BF16)
        wcols = w_ref.shape[1]
        slabs = [(c0, min(c0 + SLAB, wcols)) for c0 in range(0, wcols, SLAB)]

        def emit(n):
            for _ in range(min(n, len(slabs))):
                c0, c1 = slabs.pop(0)
                pn_ref[:, c0:c1] = _mm(hn, w_ref[:, c0:c1])

        z = _mm(p_ref[:, 2 * qk + 2 * vw:].astype(BF16), gw_ref[...]) + gb_ref[...]
        emit(1)
        la = -_softplus(-z) * (LOG2E / GLA_TAU)
        causal, tril = _tril_bf16(CHUNK)
        lane_s = lax.broadcasted_iota(jnp.int32, (qk, CHUNK), 1)
        nchunk = tl // CHUNK
        bcs_all = [_mm_sel(tril, la[c * CHUNK:(c + 1) * CHUNK]) for c in range(nchunk)]
        emit(2)
        work = []
        for c in range(nchunk):
            rows = slice(c * CHUNK, (c + 1) * CHUNK)
            bcs = bcs_all[c]
            bcs_t = bcs.T
            k_t = p_ref[rows, qk:2 * qk].T
            eprev = [jnp.zeros((1, qk), F32)] + [bcs[GLA_SUB * j - 1:GLA_SUB * j, :] for j in range(1, nsub)]
            eprev_x = jnp.concatenate([jnp.broadcast_to(e, (GLA_SUB, qk)) for e in eprev], axis=0)
            q_in = p_ref[rows, 0:qk] * scale * jnp.exp2(bcs - eprev_x)
            q_in_b = q_in.astype(BF16)
            qb = (q_in * jnp.exp2(eprev_x)).astype(BF16)
            e_last_t = bcs_t[:, CHUNK - 1:CHUNK]
            kend_t = (k_t * jnp.exp2(e_last_t - bcs_t)).astype(BF16)
            work.append((rows, bcs_t, k_t, q_in_b, qb, e_last_t, kend_t))
        srows_all = []
        for rows, bcs_t, k_t, q_in_b, qb, e_last_t, kend_t in work:
            srows = [[] for _ in range(GLA_HEADS)]
            for j in range(nsub):
                start = bcs_t[:, GLA_SUB * j - 1:GLA_SUB * j] if j else 0.0
                arg = jnp.where(lane_s < GLA_SUB * (j + 1), start - bcs_t, -jnp.inf)
                khat_t = (k_t * jnp.exp2(arg)).astype(BF16)
                for hd in range(GLA_HEADS):
                    dk = slice(hd * GLA_DK, (hd + 1) * GLA_DK)
                    srows[hd].append(_mm(q_in_b[GLA_SUB * j:GLA_SUB * (j + 1), dk], khat_t[dk, :]))
                if j == nsub // 2 - 1:
                    emit(1)
            srows_all.append(srows)
        for c, (rows, bcs_t, k_t, q_in_b, qb, e_last_t, kend_t) in enumerate(work):
            orow = slice(base + c * CHUNK, base + (c + 1) * CHUNK)
            decay = jnp.exp2(e_last_t)
            for hd in range(GLA_HEADS):
                cols = slice(hd * GLA_DV, (hd + 1) * GLA_DV)
                dk = slice(hd * GLA_DK, (hd + 1) * GLA_DK)
                scores = jnp.where(causal, jnp.concatenate(srows_all[c][hd], axis=0), 0.0).astype(BF16)
                v = p_ref[rows, 2 * qk + hd * GLA_DV: 2 * qk + (hd + 1) * GLA_DV].astype(BF16)
                state = st_ref[dk, :]
                o = _mm(scores, v) + _mm(qb[:, dk], state.astype(BF16))
                st_ref[dk, :] = decay[dk, :] * state + _mm(kend_t[dk, :], v)
                o = o * lax.rsqrt(jnp.mean(o * o, axis=-1, keepdims=True) + EPS) * gnw_ref[:, cols]
                gate = p_ref[rows, 2 * qk + vw + hd * GLA_DV: 2 * qk + vw + (hd + 1) * GLA_DV]
                g_ref[orow, cols] = (_silu(gate) * o).astype(BF16)
            emit(1 if c < nchunk - 1 else len(slabs))
        o_ref[0, base:base + tl, :] = _mm(g_ref[base:base + tl, :], wo_ref[...]).astype(o_ref.dtype)

    @pl.when(s == 0)
    def _():
        inproj(x0_ref, pa_ref)

    @pl.when((2 * s) % nt == 0)
    def _():
        st_ref[...] = jnp.zeros_like(st_ref)

    step(pa_ref, 0, xa_ref, pb_ref)
    step(pb_ref, 1, xb_ref, pa_ref)


def _gla_call(x, norm_w, w_g, gate_w, gate_b, gn_w, w_o, tl):
    bsz, seq, d = x.shape
    qk = GLA_HEADS * GLA_DK
    vw = GLA_HEADS * GLA_DV
    wcols = 2 * qk + 2 * vw + LANE
    gw = jnp.pad(gate_w, ((0, LANE - GLA_RANK), (0, 0))).astype(BF16)
    nt, steps, x_specs, out_spec = _skewed_specs(bsz, seq, tl, d)
    return pl.pallas_call(
        functools.partial(_gla_kernel, tl=tl, nt=nt),
        grid=(steps,),
        in_specs=[
            *x_specs,
            _const_spec((1, d)),
            _const_spec((d, wcols)),
            _const_spec((LANE, qk)),
            _const_spec((1, qk)),
            _const_spec((1, vw)),
            _const_spec((vw, d)),
        ],
        out_specs=out_spec,
        out_shape=jax.ShapeDtypeStruct((bsz, seq, d), BF16),
        scratch_shapes=[
            pltpu.VMEM((tl, wcols), F32),
            pltpu.VMEM((tl, wcols), F32),
            pltpu.VMEM((2 * tl, vw), BF16),
            pltpu.VMEM((qk, GLA_DV), F32),
        ],
        compiler_params=_mixer_params(),
        name="gla",
    )(x, x, x, norm_w, w_g, gw, gate_b.reshape(1, -1), gn_w, w_o)


def _merge_mlp_kernel(x_ref, nw_ref, wg_ref, gb_ref, a_ref, b_ref, c_ref, wout_ref, mnw_ref, wup_ref, wdn_ref,
                      fnw_ref, o_ref, *, final_norm):
    x = x_ref[0]
    d = x.shape[-1]
    h = _rmsnorm(x, nw_ref[...]).astype(BF16)
    merged = None
    for i, br in enumerate((a_ref, b_ref, c_ref)):
        gate = _sigmoid(_mm(h, wg_ref[:, i * d:(i + 1) * d]) + gb_ref[:, i * d:(i + 1) * d])
        term = gate * br[0].astype(F32)
        merged = term if merged is None else merged + term
    x1 = x + _mm(merged.astype(BF16), wout_ref[...])
    h2 = _rmsnorm(x1, mnw_ref[...]).astype(BF16)
    acc = x1
    for k in range(D_FF // FF_CHUNK):
        u = jnp.maximum(_mm(h2, wup_ref[:, k * FF_CHUNK:(k + 1) * FF_CHUNK]), 0.0)
        acc = acc + _mm((u * u).astype(BF16), wdn_ref[k * FF_CHUNK:(k + 1) * FF_CHUNK, :])
    if final_norm:
        acc = _rmsnorm(acc, fnw_ref[...])
    o_ref[0] = acc


def _merge_mlp_call(x, norm_w, w_mg, merge_b, ret_o, ssd_o, gla_o, w_out, mlp_norm_w, w_up, w_down,
                    final_norm_w, final_norm, tl):
    bsz, seq, d = x.shape
    tok = pl.BlockSpec((1, tl, d), lambda b, t: (b, t, 0))
    const = lambda shape: pl.BlockSpec(shape, lambda b, t: (0,) * len(shape))
    return pl.pallas_call(
        functools.partial(_merge_mlp_kernel, final_norm=final_norm),
        grid=(bsz, seq // tl),
        in_specs=[
            tok,
            const((1, d)),
            const((d, 3 * d)),
            const((1, 3 * d)),
            tok, tok, tok,
            const((d, d)),
            const((1, d)),
            const((d, D_FF)),
            const((D_FF, d)),
            const((1, d)),
        ],
        out_specs=tok,
        out_shape=jax.ShapeDtypeStruct((bsz, seq, d), F32),
        compiler_params=pltpu.CompilerParams(
            dimension_semantics=("arbitrary", "arbitrary"), vmem_limit_bytes=VMEM_LIMIT),
        name="merge_mlp",
    )(x, norm_w, w_mg, merge_b, ret_o, ssd_o, gla_o, w_out, mlp_norm_w, w_up, w_down, final_norm_w)


def _pad_cols(w, width):
    return jnp.pad(w, ((0, 0), (0, width - w.shape[1])))


def kernel(x, attn_norm_w, w_in, ret_norm_w, ret_w_o, ssd_conv_w, ssd_conv_b, ssd_dt_bias, ssd_a_log, ssd_d,
           ssd_norm_w, ssd_w_o, gla_gate_w, gla_gate_b, gla_norm_w, gla_w_o, merge_gate_b, w_out, mlp_norm_w,
           w_up, w_down, final_norm_w):
    depth = w_in.shape[0]
    seq = x.shape[1]
    tl = min(MIXER_TILE, seq // 2)
    tm = min(MLP_TILE, seq)
    inv_freq = ROPE_BASE ** (-jnp.arange(0, RET_DK, 2, dtype=F32) / RET_DK)
    ang = jnp.arange(seq, dtype=F32)[:, None] * inv_freq[None, :]
    cos = jnp.cos(ang)
    sin = jnp.sin(ang)
    cos2 = jnp.concatenate([cos, cos], axis=1)
    sin2 = jnp.concatenate([-sin, sin], axis=1)

    o_ret = 0
    o_sz = 4 * RET_HEADS * RET_DK
    o_sdt = o_sz + SSD_INNER + SSD_CONV_DIM
    o_gla = o_sdt + SSD_HEADS
    o_glr = o_gla + 2 * GLA_HEADS * GLA_DK + 2 * GLA_HEADS * GLA_DV
    o_mg = o_glr + GLA_RANK
    row = lambda v: v.reshape(1, -1)
    for layer in range(depth):
        wl = w_in[layer]
        w_r = wl[:, o_ret:o_sz].astype(BF16)
        w_dt = jnp.tile(wl[:, o_sdt:o_gla], (1, SSD_REP))
        w_s = jnp.concatenate([wl[:, o_sz:o_sdt], _pad_cols(w_dt, LANE)], axis=1).astype(BF16)
        w_g = jnp.concatenate([wl[:, o_gla:o_glr], _pad_cols(wl[:, o_glr:o_mg], LANE)], axis=1).astype(BF16)
        w_m = wl[:, o_mg:].astype(BF16)
        nw = row(attn_norm_w[layer])
        ret_o = _retention_call(x, nw, w_r, cos2, sin2, row(ret_norm_w[layer]),
                                ret_w_o[layer].astype(BF16), tl)
        ssd_o = _ssd_call(x, nw, w_s, ssd_conv_w[layer], ssd_conv_b[layer], ssd_dt_bias[layer],
                          ssd_a_log[layer], ssd_d[layer], row(ssd_norm_w[layer]),
                          ssd_w_o[layer].astype(BF16), tl)
        gla_o = _gla_call(x, nw, w_g, gla_gate_w[layer], gla_gate_b[layer], row(gla_norm_w[layer]),
                          gla_w_o[layer].astype(BF16), tl)
        x = _merge_mlp_call(x, nw, w_m, row(merge_gate_b[layer]), ret_o, ssd_o, gla_o,
                            w_out[layer].astype(BF16), row(mlp_norm_w[layer]), w_up[layer].astype(BF16),
                            w_down[layer].astype(BF16), row(final_norm_w), layer == depth - 1, tm)
    return x
```

```python
import functools
import math

import numpy as np
import jax
import jax.numpy as jnp
from jax import lax
from jax.experimental import pallas as pl
from jax.experimental.pallas import tpu as pltpu

F32 = jnp.float32
BF16 = jnp.bfloat16

D_MODEL = 1024
RET_HEADS = 4
RET_DK = 128
SSD_HEADS = 16
SSD_P = 64
SSD_GROUPS = 2
SSD_N = 128
SSD_INNER = SSD_HEADS * SSD_P
SSD_CONV = 4
SSD_CONV_DIM = SSD_INNER + 2 * SSD_GROUPS * SSD_N
GLA_HEADS = 4
GLA_DK = 64
GLA_DV = 128
GLA_RANK = 16
GLA_TAU = 16.0
GLA_SUB = 16
D_FF = 4 * D_MODEL
EPS = 1e-6
ROPE_BASE = 10000.0
LOG2E = math.log2(math.e)
CHUNK = 128
LANE = 128
SUBLANE = 8
SLAB = 256
SSD_REP = 3
MIXER_TILE = 256
MLP_TILE = 512
FF_CHUNK = 1024
VMEM_LIMIT = 56 * 1024 * 1024


def _mm(a, b):
    return lax.dot_general(a, b, (((1,), (0,)), ((), ())), preferred_element_type=F32)


def _split3(v):
    hi = v.astype(BF16)
    r1 = v - hi.astype(F32)
    mid = r1.astype(BF16)
    lo = (r1 - mid.astype(F32)).astype(BF16)
    return hi, mid, lo


def _mm_sel(sel, v):
    hi, mid, lo = _split3(v)
    return _mm(sel, hi) + _mm(sel, mid) + _mm(sel, lo)


def _rmsnorm(x, w):
    return x * lax.rsqrt(jnp.mean(x * x, axis=-1, keepdims=True) + EPS) * w


def _sigmoid(x):
    return 0.5 + 0.5 * jnp.tanh(0.5 * x)


def _silu(x):
    h = 0.5 * x
    return h + h * jnp.tanh(h)


def _softplus(x):
    return jnp.maximum(x, 0.0) + jnp.log(1.0 + jnp.exp(-jnp.abs(x)))


def _tril_bf16(n):
    r = lax.broadcasted_iota(jnp.int32, (n, n), 0)
    c = lax.broadcasted_iota(jnp.int32, (n, n), 1)
    return r >= c, (r >= c).astype(BF16)


def _const_spec(shape):
    nd = len(shape)
    return pl.BlockSpec(shape, lambda s: (0,) * nd)


def _skewed_specs(bsz, seq, tl, d):
    nt = seq // tl
    assert seq % tl == 0 and nt % 2 == 0
    total = bsz * nt
    tile = lambda g: (g // nt, g % nt, 0)
    x_first = pl.BlockSpec((1, tl, d), lambda s: (0, 0, 0))
    x_odd = pl.BlockSpec((1, tl, d), lambda s: tile(2 * s + 1))
    x_next = pl.BlockSpec((1, tl, d), lambda s: tile(jnp.minimum(2 * s + 2, total - 1)))
    out = pl.BlockSpec((1, 2 * tl, d), lambda s: ((2 * s) // nt, ((2 * s) % nt) // 2, 0))
    return nt, total // 2, (x_first, x_odd, x_next), out


def _slab_emitter(hn, w_ref, pn_ref):
    wcols = w_ref.shape[1]
    slabs = [(c0, min(c0 + SLAB, wcols)) for c0 in range(0, wcols, SLAB)]

    def emit(n=None):
        for _ in range(len(slabs) if n is None else min(n, len(slabs))):
            c0, c1 = slabs.pop(0)
            pn_ref[:, c0:c1] = _mm(hn, w_ref[:, c0:c1])

    return emit, len(slabs)


def _spread(total, points):
    return [total * (i + 1) // points - total * i // points for i in range(points)]


def _gated_out(y, wo_ref, p_ref, mb_ref, col0):
    d = wo_ref.shape[1]
    return _sigmoid(p_ref[:, col0:col0 + d] + mb_ref[...]) * _mm(y, wo_ref[...])


def _mixer_params():
    return pltpu.CompilerParams(dimension_semantics=("arbitrary",), vmem_limit_bytes=VMEM_LIMIT)


def _ret_kernel(x0_ref, xa_ref, xb_ref, nw_ref, w_ref, cos_ref, sin_ref, dec_ref, qd_ref, kd_ref, gnw_ref,
                wo_ref, mb_ref, o_ref, pa_ref, pb_ref, g_ref, st_ref, *, tl, nt, chunk_decay):
    s = pl.program_id(0)
    hw = RET_HEADS * RET_DK

    def inproj(x_ref, p_ref):
        p_ref[...] = _mm(_rmsnorm(x_ref[0], nw_ref[...]).astype(BF16), w_ref[...])

    def step(p_ref, half, xn_ref, pn_ref):
        base = half * tl
        hn = _rmsnorm(xn_ref[0], nw_ref[...]).astype(BF16)
        emit, nslab = _slab_emitter(hn, w_ref, pn_ref)
        units = [(c, hd) for c in range(tl // CHUNK) for hd in range(RET_HEADS)]
        per_unit = _spread(nslab, len(units))
        for u, (c, hd) in enumerate(units):
            rows = slice(c * CHUNK, (c + 1) * CHUNK)
            orow = slice(base + c * CHUNK, base + (c + 1) * CHUNK)
            cols = slice(hd * RET_DK, (hd + 1) * RET_DK)
            cos = cos_ref[orow, :]
            sin = sin_ref[orow, :]
            q = p_ref[rows, cols]
            k = p_ref[rows, hw + hd * RET_DK: hw + (hd + 1) * RET_DK]
            v = p_ref[rows, 2 * hw + hd * RET_DK: 2 * hw + (hd + 1) * RET_DK].astype(BF16)
            q = q * cos + pltpu.roll(q, RET_DK // 2, 1) * sin
            k = k * cos + pltpu.roll(k, RET_DK // 2, 1) * sin
            k_t = k.T
            scores = _mm(q.astype(BF16), k_t.astype(BF16))
            kv = _mm((k_t * kd_ref[cols, :]).astype(BF16), v)
            emit(per_unit[u])
            state = st_ref[hd]
            o = (_mm((scores * dec_ref[hd]).astype(BF16), v)
                 + _mm((q * qd_ref[:, cols]).astype(BF16), state.astype(BF16)))
            st_ref[hd] = chunk_decay[hd] * state + kv
            o = o * lax.rsqrt(jnp.mean(o * o, axis=-1, keepdims=True) + EPS) * gnw_ref[:, cols]
            gate = p_ref[rows, 3 * hw + hd * RET_DK: 3 * hw + (hd + 1) * RET_DK]
            g_ref[orow, cols] = (_silu(gate) * o).astype(BF16)
        o_ref[0, base:base + tl, :] = _gated_out(g_ref[base:base + tl, :], wo_ref, p_ref, mb_ref,
                                                 4 * hw).astype(o_ref.dtype)

    @pl.when(s == 0)
    def _():
        inproj(x0_ref, pa_ref)

    @pl.when((2 * s) % nt == 0)
    def _():
        st_ref[...] = jnp.zeros_like(st_ref)

    step(pa_ref, 0, xa_ref, pb_ref)
    step(pb_ref, 1, xb_ref, pa_ref)


def _retention_tables():
    lg = np.log1p(-np.exp2(-5.0 - np.arange(RET_HEADS, dtype=np.float64)))
    pos = np.arange(CHUNK, dtype=np.float64)
    scale = RET_DK ** -0.5
    dist = pos[:, None] - pos[None, :]
    dec = np.where(dist >= 0, np.exp(lg[:, None, None] * np.maximum(dist, 0.0)), 0.0) * scale
    qd = np.repeat(np.exp(lg[None, :] * (pos[:, None] + 1.0)), RET_DK, axis=1)
    kd = np.repeat(np.exp(lg[:, None] * (CHUNK - 1.0 - pos[None, :])), RET_DK, axis=0) * scale
    cd = tuple(float(v) for v in np.exp(lg * CHUNK))
    return (jnp.asarray(dec, F32), jnp.asarray(qd, F32), jnp.asarray(kd, F32), cd)


def _retention_call(x, norm_w, w_r, cos2, sin2, gn_w, w_o, merge_b, tl):
    bsz, seq, d = x.shape
    dec, qd, kd, cd = _retention_tables()
    hw = RET_HEADS * RET_DK
    nt, steps, x_specs, out_spec = _skewed_specs(bsz, seq, tl, d)
    rope = pl.BlockSpec((2 * tl, RET_DK), lambda s: (((2 * s) % nt) // 2, 0))
    return pl.pallas_call(
        functools.partial(_ret_kernel, tl=tl, nt=nt, chunk_decay=cd),
        grid=(steps,),
        in_specs=[
            *x_specs,
            _const_spec((1, d)),
            _const_spec((d, 4 * hw + d)),
            rope, rope,
            _const_spec((RET_HEADS, CHUNK, CHUNK)),
            _const_spec((CHUNK, hw)),
            _const_spec((hw, CHUNK)),
            _const_spec((1, hw)),
            _const_spec((hw, d)),
            _const_spec((1, d)),
        ],
        out_specs=out_spec,
        out_shape=jax.ShapeDtypeStruct((bsz, seq, d), BF16),
        scratch_shapes=[
            pltpu.VMEM((tl, 4 * hw + d), F32),
            pltpu.VMEM((tl, 4 * hw + d), F32),
            pltpu.VMEM((2 * tl, hw), BF16),
            pltpu.VMEM((RET_HEADS, RET_DK, RET_DK), F32),
        ],
        compiler_params=_mixer_params(),
        name="retention",
    )(x, x, x, norm_w, w_r, cos2, sin2, dec, qd, kd, gn_w, w_o, merge_b)


def _ssd_kernel(x0_ref, xa_ref, xb_ref, nw_ref, w_ref, cw_ref, cb_ref, dtb_ref, alog_ref, dsk_ref, ex_ref,
                gnw_ref, wo_ref, mb_ref, o_ref, pa_ref, pb_ref, hx_ref, hb_ref, xc_ref, y_ref, st_ref, *, tl, nt):
    s = pl.program_id(0)
    gw = SSD_INNER // SSD_GROUPS
    hpg = SSD_HEADS // SSD_GROUPS
    xbc = slice(SSD_INNER, SSD_INNER + SSD_CONV_DIM)

    def inproj(x_ref, p_ref):
        p_ref[...] = _mm(_rmsnorm(x_ref[0], nw_ref[...]).astype(BF16), w_ref[...])

    def shift_rows(v, n, hist):
        r = pltpu.roll(v, n, 0)
        sub = lax.broadcasted_iota(jnp.int32, (SUBLANE, v.shape[1]), 0)
        head = jnp.where(sub < n, pltpu.roll(hist, n, 0), r[0:SUBLANE])
        return jnp.concatenate([head, r[SUBLANE:]], axis=0)

    def step(p_ref, half, xn_ref, pn_ref):
        base = half * tl
        hn = _rmsnorm(xn_ref[0], nw_ref[...]).astype(BF16)
        emit, _ = _slab_emitter(hn, w_ref, pn_ref)
        lane = lax.broadcasted_iota(jnp.int32, (CHUNK, LANE), 1)

        def expand(v):
            hi, mid, lo = _split3(v)
            packed = jnp.where(lane < SSD_HEADS, hi, jnp.where(lane < 2 * SSD_HEADS, mid, lo))
            return _mm(packed, ex_ref[...])

        def conv(rows):
            xin = p_ref[rows, xbc]
            x1 = shift_rows(xin, 1, hx_ref[...])
            hx_ref[...] = xin[CHUNK - SUBLANE:, :]
            near = cb_ref[...] + cw_ref[3:4, :] * xin + cw_ref[2:3, :] * x1
            far = cw_ref[1:2, :] * xin + cw_ref[0:1, :] * x1
            xc_ref[rows, :] = _silu(near + shift_rows(far, 2, hb_ref[...]))
            hb_ref[...] = far[CHUNK - SUBLANE:, :]

        dt = _softplus(p_ref[:, SSD_INNER + SSD_CONV_DIM:SSD_INNER + SSD_CONV_DIM + LANE]
                       + dtb_ref[...])
        a = dt * (-jnp.exp(alog_ref[...]))
        causal, tril = _tril_bf16(CHUNK)
        zero_b = jnp.zeros((CHUNK, LANE), BF16)
        nchunk = tl // CHUNK
        for c in range(nchunk):
            rows = slice(c * CHUNK, (c + 1) * CHUNK)
            orow = slice(base + c * CHUNK, base + (c + 1) * CHUNK)
            conv(rows)
            emit(1)
            a_cs = _mm_sel(tril, a[rows])
            dt_t = dt[rows].T
            emit(1)
            a_last = a_cs[CHUNK - 1:CHUNK, :]
            a2 = a_cs * LOG2E
            a2t = a2.T
            ecs = expand(jnp.exp(a_cs))
            dend = expand(dt[rows] * jnp.exp(a_last - a_cs))
            emit(1)
            sx = xc_ref[rows, 0:SSD_INNER]
            xs_b = sx.astype(BF16)
            xd_b = (sx * dend).astype(BF16)
            for g in range(SSD_GROUPS):
                gc = slice(g * gw, (g + 1) * gw)
                bmat_t = xc_ref[rows, SSD_INNER + g * SSD_N: SSD_INNER + (g + 1) * SSD_N].T.astype(BF16)
                cmat = xc_ref[rows, SSD_INNER + (SSD_GROUPS + g) * SSD_N:
                              SSD_INNER + (SSD_GROUPS + g + 1) * SSD_N].astype(BF16)
                cb = _mm(cmat, bmat_t)
                state = st_ref[:, gc]
                y_off = _mm(cmat, state.astype(BF16))
                upd = _mm(bmat_t, xd_b[:, gc])
                emit(1)
                cb = jnp.where(causal, cb, 0.0)
                pieces = []
                for r in range(0, hpg, 2):
                    hh = g * hpg + r
                    pair = xs_b[:, hh * SSD_P:(hh + 2) * SSD_P]
                    rhs = jnp.concatenate([jnp.where(lane < SSD_P, pair, zero_b),
                                           jnp.where(lane >= SSD_P, pair, zero_b)], axis=0)
                    lhs = []
                    for h2 in (hh, hh + 1):
                        seg = jnp.minimum(a2[:, h2:h2 + 1] - a2t[h2:h2 + 1, :], 0.0)
                        lhs.append((cb * jnp.exp2(seg) * dt_t[h2:h2 + 1, :]).astype(BF16))
                    pieces.append(_mm(jnp.concatenate(lhs, axis=1), rhs))
                y_diag = jnp.concatenate(pieces, axis=1)
                y = y_diag + y_off * ecs[:, gc] + sx[:, gc] * dsk_ref[:, gc]
                st_ref[:, gc] = ecs[CHUNK - 1:CHUNK, gc] * state + upd
                y = y * _silu(p_ref[rows, gc])
                y = y * lax.rsqrt(jnp.mean(y * y, axis=-1, keepdims=True) + EPS) * gnw_ref[:, gc]
                y_ref[orow, gc] = y.astype(BF16)
                emit(1 if c < nchunk - 1 or g < SSD_GROUPS - 1 else None)
        o_ref[0, base:base + tl, :] = _gated_out(y_ref[base:base + tl, :], wo_ref, p_ref, mb_ref,
                                                 SSD_INNER + SSD_CONV_DIM + LANE).astype(o_ref.dtype)

    @pl.when(s == 0)
    def _():
        inproj(x0_ref, pa_ref)

    @pl.when((2 * s) % nt == 0)
    def _():
        st_ref[...] = jnp.zeros_like(st_ref)
        hx_ref[...] = jnp.zeros_like(hx_ref)
        hb_ref[...] = jnp.zeros_like(hb_ref)

    step(pa_ref, 0, xa_ref, pb_ref)
    step(pb_ref, 1, xb_ref, pa_ref)


def _ssd_call(x, norm_w, w_s, conv_w, conv_b, dt_bias, a_log, d_skip, gn_w, w_o, merge_b, tl):
    bsz, seq, d = x.shape
    wcols = SSD_INNER + SSD_CONV_DIM + LANE + d
    pad = LANE - SSD_REP * SSD_HEADS
    dtb = jnp.pad(jnp.tile(dt_bias, SSD_REP), (0, pad)).reshape(1, LANE)
    alog = jnp.pad(jnp.tile(a_log, SSD_REP), (0, pad)).reshape(1, LANE)
    dsk = jnp.repeat(d_skip, SSD_P).reshape(1, SSD_INNER)
    expand = np.zeros((LANE, SSD_INNER), np.float32)
    for r in range(SSD_REP * SSD_HEADS):
        hh = r % SSD_HEADS
        expand[r, hh * SSD_P:(hh + 1) * SSD_P] = 1.0
    expand = jnp.asarray(expand, BF16)
    nt, steps, x_specs, out_spec = _skewed_specs(bsz, seq, tl, d)
    return pl.pallas_call(
        functools.partial(_ssd_kernel, tl=tl, nt=nt),
        grid=(steps,),
        in_specs=[
            *x_specs,
            _const_spec((1, d)),
            _const_spec((d, wcols)),
            _const_spec((SSD_CONV, SSD_CONV_DIM)),
            _const_spec((1, SSD_CONV_DIM)),
            _const_spec((1, LANE)),
            _const_spec((1, LANE)),
            _const_spec((1, SSD_INNER)),
            _const_spec((LANE, SSD_INNER)),
            _const_spec((1, SSD_INNER)),
            _const_spec((SSD_INNER, d)),
            _const_spec((1, d)),
        ],
        out_specs=out_spec,
        out_shape=jax.ShapeDtypeStruct((bsz, seq, d), BF16),
        scratch_shapes=[
            pltpu.VMEM((tl, wcols), F32),
            pltpu.VMEM((tl, wcols), F32),
            pltpu.VMEM((SUBLANE, SSD_CONV_DIM), F32),
            pltpu.VMEM((SUBLANE, SSD_CONV_DIM), F32),
            pltpu.VMEM((tl, SSD_CONV_DIM), F32),
            pltpu.VMEM((2 * tl, SSD_INNER), BF16),
            pltpu.VMEM((SSD_N, SSD_INNER), F32),
        ],
        compiler_params=_mixer_params(),
        name="ssd",
    )(x, x, x, norm_w, w_s, conv_w, conv_b.reshape(1, -1), dtb, alog, dsk, expand, gn_w, w_o, merge_b)


def _gla_kernel(x0_ref, xa_ref, xb_ref, nw_ref, w_ref, gw_ref, gb_ref, gnw_ref, wo_ref, mb_ref,
                o_ref, pa_ref, pb_ref, g_ref, st_ref, *, tl, nt):
    s = pl.program_id(0)
    qk = GLA_HEADS * GLA_DK
    vw = GLA_HEADS * GLA_DV
    nsub = CHUNK // GLA_SUB
    scale = GLA_DK ** -0.5

    lr0 = 2 * qk + 2 * vw

    def inproj(x_ref, p_ref):
        p_ref[...] = _mm(_rmsnorm(x_ref[0], nw_ref[...]).astype(BF16), w_ref[...])

    def step(p_ref, half, xn_ref, pn_ref):
        base = half * tl
        hn = _rmsnorm(xn_ref[0], nw_ref[...]).astype(BF16)
        emit, _ = _slab_emitter(hn, w_ref, pn_ref)
        z = _mm(p_ref[:, lr0:lr0 + LANE].astype(BF16), gw_ref[...]) + gb_ref[...]
        emit(2)
        la = -_softplus(-z) * (LOG2E / GLA_TAU)
        causal, tril = _tril_bf16(CHUNK)
        lane_s = lax.broadcasted_iota(jnp.int32, (qk, CHUNK), 1)
        nchunk = tl // CHUNK
        bcs_all = [_mm_sel(tril, la[c * CHUNK:(c + 1) * CHUNK]) for c in range(nchunk)]
        emit(3)
        work = []
        for c in range(nchunk):
            rows = slice(c * CHUNK, (c + 1) * CHUNK)
            bcs = bcs_all[c]
            bcs_t = bcs.T
            k_t = p_ref[rows, qk:2 * qk].T
            eprev = [jnp.zeros((1, qk), F32)] + [bcs[GLA_SUB * j - 1:GLA_SUB * j, :] for j in range(1, nsub)]
            eprev_x = jnp.concatenate([jnp.broadcast_to(e, (GLA_SUB, qk)) for e in eprev], axis=0)
            q_in = p_ref[rows, 0:qk] * scale * jnp.exp2(bcs - eprev_x)
            q_in_b = q_in.astype(BF16)
            qb = (q_in * jnp.exp2(eprev_x)).astype(BF16)
            e_last_t = bcs_t[:, CHUNK - 1:CHUNK]
            kend_t = (k_t * jnp.exp2(e_last_t - bcs_t)).astype(BF16)
            work.append((rows, bcs_t, k_t, q_in_b, qb, e_last_t, kend_t))
        srows_all = []
        for rows, bcs_t, k_t, q_in_b, qb, e_last_t, kend_t in work:
            srows = [[] for _ in range(GLA_HEADS)]
            for j in range(nsub):
                start = bcs_t[:, GLA_SUB * j - 1:GLA_SUB * j] if j else 0.0
                arg = jnp.where(lane_s < GLA_SUB * (j + 1), start - bcs_t, -jnp.inf)
                khat_t = (k_t * jnp.exp2(arg)).astype(BF16)
                for hd in range(GLA_HEADS):
                    dk = slice(hd * GLA_DK, (hd + 1) * GLA_DK)
                    srows[hd].append(_mm(q_in_b[GLA_SUB * j:GLA_SUB * (j + 1), dk], khat_t[dk, :]))
                if j == nsub // 2 - 1:
                    emit(1)
            srows_all.append(srows)
        for c, (rows, bcs_t, k_t, q_in_b, qb, e_last_t, kend_t) in enumerate(work):
            orow = slice(base + c * CHUNK, base + (c + 1) * CHUNK)
            decay = jnp.exp2(e_last_t)
            for hd in range(GLA_HEADS):
                cols = slice(hd * GLA_DV, (hd + 1) * GLA_DV)
                dk = slice(hd * GLA_DK, (hd + 1) * GLA_DK)
                scores = jnp.where(causal, jnp.concatenate(srows_all[c][hd], axis=0), 0.0).astype(BF16)
                v = p_ref[rows, 2 * qk + hd * GLA_DV: 2 * qk + (hd + 1) * GLA_DV].astype(BF16)
                state = st_ref[dk, :]
                o = _mm(scores, v) + _mm(qb[:, dk], state.astype(BF16))
                st_ref[dk, :] = decay[dk, :] * state + _mm(kend_t[dk, :], v)
                o = o * lax.rsqrt(jnp.mean(o * o, axis=-1, keepdims=True) + EPS) * gnw_ref[:, cols]
                gate = p_ref[rows, 2 * qk + vw + hd * GLA_DV: 2 * qk + vw + (hd + 1) * GLA_DV]
                g_ref[orow, cols] = (_silu(gate) * o).astype(BF16)
            emit(2 if c < nchunk - 1 else None)
        o_ref[0, base:base + tl, :] = _gated_out(g_ref[base:base + tl, :], wo_ref, p_ref, mb_ref,
                                                 lr0 + LANE).astype(o_ref.dtype)

    @pl.when(s == 0)
    def _():
        inproj(x0_ref, pa_ref)

    @pl.when((2 * s) % nt == 0)
    def _():
        st_ref[...] = jnp.zeros_like(st_ref)

    step(pa_ref, 0, xa_ref, pb_ref)
    step(pb_ref, 1, xb_ref, pa_ref)


def _gla_call(x, norm_w, w_g, gate_w, gate_b, gn_w, w_o, merge_b, tl):
    bsz, seq, d = x.shape
    qk = GLA_HEADS * GLA_DK
    vw = GLA_HEADS * GLA_DV
    wcols = 2 * qk + 2 * vw + LANE + d
    gw = jnp.pad(gate_w, ((0, LANE - GLA_RANK), (0, 0))).astype(BF16)
    nt, steps, x_specs, out_spec = _skewed_specs(bsz, seq, tl, d)
    return pl.pallas_call(
        functools.partial(_gla_kernel, tl=tl, nt=nt),
        grid=(steps,),
        in_specs=[
            *x_specs,
            _const_spec((1, d)),
            _const_spec((d, wcols)),
            _const_spec((LANE, qk)),
            _const_spec((1, qk)),
            _const_spec((1, vw)),
            _const_spec((vw, d)),
            _const_spec((1, d)),
        ],
        out_specs=out_spec,
        out_shape=jax.ShapeDtypeStruct((bsz, seq, d), BF16),
        scratch_shapes=[
            pltpu.VMEM((tl, wcols), F32),
            pltpu.VMEM((tl, wcols), F32),
            pltpu.VMEM((2 * tl, vw), BF16),
            pltpu.VMEM((qk, GLA_DV), F32),
        ],
        compiler_params=_mixer_params(),
        name="gla",
    )(x, x, x, norm_w, w_g, gw, gate_b.reshape(1, -1), gn_w, w_o, merge_b)


def _merge_mlp_kernel(x_ref, a_ref, b_ref, c_ref, wout_ref, mnw_ref, wup_ref, wdn_ref, fnw_ref, o_ref, *,
                      final_norm):
    x = x_ref[0]
    merged = a_ref[0].astype(F32) + b_ref[0].astype(F32) + c_ref[0].astype(F32)
    x1 = x + _mm(merged.astype(BF16), wout_ref[...])
    h2 = _rmsnorm(x1, mnw_ref[...]).astype(BF16)
    acc = x1
    for k in range(D_FF // FF_CHUNK):
        u = jnp.maximum(_mm(h2, wup_ref[:, k * FF_CHUNK:(k + 1) * FF_CHUNK]), 0.0)
        acc = acc + _mm((u * u).astype(BF16), wdn_ref[k * FF_CHUNK:(k + 1) * FF_CHUNK, :])
    if final_norm:
        acc = _rmsnorm(acc, fnw_ref[...])
    o_ref[0] = acc


def _merge_mlp_call(x, ret_o, ssd_o, gla_o, w_out, mlp_norm_w, w_up, w_down, final_norm_w, final_norm, tl):
    bsz, seq, d = x.shape
    tok = pl.BlockSpec((1, tl, d), lambda b, t: (b, t, 0))
    const = lambda shape: pl.BlockSpec(shape, lambda b, t: (0,) * len(shape))
    return pl.pallas_call(
        functools.partial(_merge_mlp_kernel, final_norm=final_norm),
        grid=(bsz, seq // tl),
        in_specs=[
            tok, tok, tok, tok,
            const((d, d)),
            const((1, d)),
            const((d, D_FF)),
            const((D_FF, d)),
            const((1, d)),
        ],
        out_specs=tok,
        out_shape=jax.ShapeDtypeStruct((bsz, seq, d), F32),
        compiler_params=pltpu.CompilerParams(
            dimension_semantics=("arbitrary", "arbitrary"), vmem_limit_bytes=VMEM_LIMIT),
        name="merge_mlp",
    )(x, ret_o, ssd_o, gla_o, w_out, mlp_norm_w, w_up, w_down, final_norm_w)


def _pad_cols(w, width):
    return jnp.pad(w, ((0, 0), (0, width - w.shape[1])))


def kernel(x, attn_norm_w, w_in, ret_norm_w, ret_w_o, ssd_conv_w, ssd_conv_b, ssd_dt_bias, ssd_a_log, ssd_d,
           ssd_norm_w, ssd_w_o, gla_gate_w, gla_gate_b, gla_norm_w, gla_w_o, merge_gate_b, w_out, mlp_norm_w,
           w_up, w_down, final_norm_w):
    depth = w_in.shape[0]
    seq = x.shape[1]
    d = x.shape[2]
    tl = min(MIXER_TILE, seq // 2)
    tm = min(MLP_TILE, seq)
    inv_freq = ROPE_BASE ** (-jnp.arange(0, RET_DK, 2, dtype=F32) / RET_DK)
    ang = jnp.arange(seq, dtype=F32)[:, None] * inv_freq[None, :]
    cos = jnp.cos(ang)
    sin = jnp.sin(ang)
    cos2 = jnp.concatenate([cos, cos], axis=1)
    sin2 = jnp.concatenate([-sin, sin], axis=1)

    o_ret = 0
    o_sz = 4 * RET_HEADS * RET_DK
    o_sdt = o_sz + SSD_INNER + SSD_CONV_DIM
    o_gla = o_sdt + SSD_HEADS
    o_glr = o_gla + 2 * GLA_HEADS * GLA_DK + 2 * GLA_HEADS * GLA_DV
    o_mg = o_glr + GLA_RANK
    row = lambda v: v.reshape(1, -1)
    for layer in range(depth):
        wl = w_in[layer]
        mg = [wl[:, o_mg + i * d:o_mg + (i + 1) * d] for i in range(3)]
        mb = [row(merge_gate_b[layer, i * d:(i + 1) * d]) for i in range(3)]
        w_r = jnp.concatenate([wl[:, o_ret:o_sz], mg[0]], axis=1).astype(BF16)
        w_dt = jnp.tile(wl[:, o_sdt:o_gla], (1, SSD_REP))
        w_s = jnp.concatenate([wl[:, o_sz:o_sdt], _pad_cols(w_dt, LANE), mg[1]], axis=1).astype(BF16)
        w_g = jnp.concatenate([wl[:, o_gla:o_glr], _pad_cols(wl[:, o_glr:o_mg], LANE), mg[2]],
                              axis=1).astype(BF16)
        nw = row(attn_norm_w[layer])
        ret_o = _retention_call(x, nw, w_r, cos2, sin2, row(ret_norm_w[layer]),
                                ret_w_o[layer].astype(BF16), mb[0], tl)
        ssd_o = _ssd_call(x, nw, w_s, ssd_conv_w[layer], ssd_conv_b[layer], ssd_dt_bias[layer],
                          ssd_a_log[layer], ssd_d[layer], row(ssd_norm_w[layer]),
                          ssd_w_o[layer].astype(BF16), mb[1], tl)
        gla_o = _gla_call(x, nw, w_g, gla_gate_w[layer], gla_gate_b[layer], row(gla_norm_w[layer]),
                          gla_w_o[layer].astype(BF16), mb[2], tl)
        x = _merge_mlp_call(x, ret_o, ssd_o, gla_o, w_out[layer].astype(BF16), row(mlp_norm_w[layer]),
                            w_up[layer].astype(BF16), w_down[layer].astype(BF16), row(final_norm_w),
                            layer == depth - 1, tm)
    return x
```

```python
import functools
import math

import numpy as np
import jax
import jax.numpy as jnp
from jax import lax
from jax.experimental import pallas as pl
from jax.experimental.pallas import tpu as pltpu

F32 = jnp.float32
BF16 = jnp.bfloat16

D_MODEL = 1024
RET_HEADS = 4
RET_DK = 128
SSD_HEADS = 16
SSD_P = 64
SSD_GROUPS = 2
SSD_N = 128
SSD_INNER = SSD_HEADS * SSD_P
SSD_CONV = 4
SSD_CONV_DIM = SSD_INNER + 2 * SSD_GROUPS * SSD_N
GLA_HEADS = 4
GLA_DK = 64
GLA_DV = 128
GLA_RANK = 16
GLA_TAU = 16.0
GLA_SUB = 16
D_FF = 4 * D_MODEL
EPS = 1e-6
ROPE_BASE = 10000.0
LOG2E = math.log2(math.e)
CHUNK = 128
LANE = 128
SUBLANE = 8
SLAB = 256
SSD_REP = 3
MIXER_TILE = 256
TILES_PER_STEP = 8
MLP_TILE = 512
MLP_BLOCK = 1024
FF_CHUNK = 1024
VMEM_LIMIT = 56 * 1024 * 1024


def _mm(a, b):
    return lax.dot_general(a, b, (((1,), (0,)), ((), ())), preferred_element_type=F32)


def _split3(v):
    hi = v.astype(BF16)
    r1 = v - hi.astype(F32)
    mid = r1.astype(BF16)
    lo = (r1 - mid.astype(F32)).astype(BF16)
    return hi, mid, lo


def _mm_sel(sel, v):
    hi, mid, lo = _split3(v)
    return _mm(sel, hi) + _mm(sel, mid) + _mm(sel, lo)


def _rmsnorm(x, w):
    return x * lax.rsqrt(jnp.mean(x * x, axis=-1, keepdims=True) + EPS) * w


def _sigmoid(x):
    return 0.5 + 0.5 * jnp.tanh(0.5 * x)


def _silu(x):
    h = 0.5 * x
    return h + h * jnp.tanh(h)


def _softplus(x):
    return jnp.maximum(x, 0.0) + jnp.log(1.0 + jnp.exp(-jnp.abs(x)))


def _tril_bf16(n):
    r = lax.broadcasted_iota(jnp.int32, (n, n), 0)
    c = lax.broadcasted_iota(jnp.int32, (n, n), 1)
    return r >= c, (r >= c).astype(BF16)


def _const_spec(shape):
    nd = len(shape)
    return pl.BlockSpec(shape, lambda s: (0,) * nd)


def _skewed_specs(bsz, seq, tl, d):
    nt = seq // tl
    tpb = math.gcd(nt, TILES_PER_STEP)
    assert seq % tl == 0 and tpb % 2 == 0
    nb = nt // tpb
    last_tile = bsz * nt - 1
    block = pl.BlockSpec((1, tpb * tl, d), lambda s: (s // nb, s % nb, 0))

    def next_tile(s):
        g = jnp.minimum((s + 1) * tpb, last_tile)
        return (g // nt, g % nt, 0)

    x_next = pl.BlockSpec((1, tl, d), next_tile)
    return tpb, nb, bsz * nb, (block, x_next), block


def _run_skewed(step, inproj, reset, x_ref, xn_ref, pa_ref, pb_ref, *, tl, tpb, nb):
    s = pl.program_id(0)

    @pl.when(s == 0)
    def _():
        inproj(x_ref[0, 0:tl, :], pa_ref)

    @pl.when(s % nb == 0)
    def _():
        reset()

    def pair(i, carry):
        k0 = 2 * i
        r1 = pl.multiple_of((k0 + 1) * tl, tl)
        step(pa_ref, 0, x_ref[0, pl.ds(r1, tl), :], pb_ref, pl.multiple_of(k0 * tl, tl))
        r2 = pl.multiple_of(jnp.minimum(k0 + 2, tpb - 1) * tl, tl)
        xn = jnp.where(i == tpb // 2 - 1, xn_ref[0], x_ref[0, pl.ds(r2, tl), :])
        step(pb_ref, 1, xn, pa_ref, pl.multiple_of((k0 + 1) * tl, tl))
        return carry

    lax.fori_loop(0, tpb // 2, pair, 0)


def _slab_emitter(hn, w_ref, pn_ref):
    wcols = w_ref.shape[1]
    slabs = [(c0, min(c0 + SLAB, wcols)) for c0 in range(0, wcols, SLAB)]

    def emit(n=None):
        for _ in range(len(slabs) if n is None else min(n, len(slabs))):
            c0, c1 = slabs.pop(0)
            pn_ref[:, c0:c1] = _mm(hn, w_ref[:, c0:c1])

    return emit, len(slabs)


def _spread(total, points):
    return [total * (i + 1) // points - total * i // points for i in range(points)]


def _gated_out(y, wo_ref, p_ref, mb_ref, col0):
    d = wo_ref.shape[1]
    return _sigmoid(p_ref[:, col0:col0 + d] + mb_ref[...]) * _mm(y, wo_ref[...])


def _mixer_params():
    return pltpu.CompilerParams(dimension_semantics=("arbitrary",), vmem_limit_bytes=VMEM_LIMIT)


def _ret_kernel(x_ref, xn_ref, nw_ref, w_ref, cos_ref, sin_ref, dec_ref, qd_ref, kd_ref, gnw_ref,
                wo_ref, mb_ref, o_ref, pa_ref, pb_ref, g_ref, st_ref, *, tl, tpb, nb, chunk_decay):
    hw = RET_HEADS * RET_DK

    def inproj(x, p_ref):
        p_ref[...] = _mm(_rmsnorm(x, nw_ref[...]).astype(BF16), w_ref[...])

    def step(p_ref, slot, xn, pn_ref, out0):
        base = slot * tl
        hn = _rmsnorm(xn, nw_ref[...]).astype(BF16)
        emit, nslab = _slab_emitter(hn, w_ref, pn_ref)
        units = [(c, hd) for c in range(tl // CHUNK) for hd in range(RET_HEADS)]
        per_unit = _spread(nslab, len(units))
        for u, (c, hd) in enumerate(units):
            rows = slice(c * CHUNK, (c + 1) * CHUNK)
            orow = slice(base + c * CHUNK, base + (c + 1) * CHUNK)
            cols = slice(hd * RET_DK, (hd + 1) * RET_DK)
            cos = cos_ref[pl.ds(out0 + c * CHUNK, CHUNK), :]
            sin = sin_ref[pl.ds(out0 + c * CHUNK, CHUNK), :]
            q = p_ref[rows, cols]
            k = p_ref[rows, hw + hd * RET_DK: hw + (hd + 1) * RET_DK]
            v = p_ref[rows, 2 * hw + hd * RET_DK: 2 * hw + (hd + 1) * RET_DK].astype(BF16)
            q = q * cos + pltpu.roll(q, RET_DK // 2, 1) * sin
            k = k * cos + pltpu.roll(k, RET_DK // 2, 1) * sin
            k_t = k.T
            scores = _mm(q.astype(BF16), k_t.astype(BF16))
            kv = _mm((k_t * kd_ref[cols, :]).astype(BF16), v)
            emit(per_unit[u])
            state = st_ref[hd]
            o = (_mm((scores * dec_ref[hd]).astype(BF16), v)
                 + _mm((q * qd_ref[:, cols]).astype(BF16), state.astype(BF16)))
            st_ref[hd] = chunk_decay[hd] * state + kv
            o = o * lax.rsqrt(jnp.mean(o * o, axis=-1, keepdims=True) + EPS) * gnw_ref[:, cols]
            gate = p_ref[rows, 3 * hw + hd * RET_DK: 3 * hw + (hd + 1) * RET_DK]
            g_ref[orow, cols] = (_silu(gate) * o).astype(BF16)
        o_ref[0, pl.ds(out0, tl), :] = _gated_out(g_ref[base:base + tl, :], wo_ref, p_ref, mb_ref,
                                                  4 * hw).astype(o_ref.dtype)

    def reset():
        st_ref[...] = jnp.zeros_like(st_ref)

    _run_skewed(step, inproj, reset, x_ref, xn_ref, pa_ref, pb_ref, tl=tl, tpb=tpb, nb=nb)


def _retention_tables():
    lg = np.log1p(-np.exp2(-5.0 - np.arange(RET_HEADS, dtype=np.float64)))
    pos = np.arange(CHUNK, dtype=np.float64)
    scale = RET_DK ** -0.5
    dist = pos[:, None] - pos[None, :]
    dec = np.where(dist >= 0, np.exp(lg[:, None, None] * np.maximum(dist, 0.0)), 0.0) * scale
    qd = np.repeat(np.exp(lg[None, :] * (pos[:, None] + 1.0)), RET_DK, axis=1)
    kd = np.repeat(np.exp(lg[:, None] * (CHUNK - 1.0 - pos[None, :])), RET_DK, axis=0) * scale
    cd = tuple(float(v) for v in np.exp(lg * CHUNK))
    return (jnp.asarray(dec, F32), jnp.asarray(qd, F32), jnp.asarray(kd, F32), cd)


def _retention_call(x, norm_w, w_r, cos2, sin2, gn_w, w_o, merge_b, tl):
    bsz, seq, d = x.shape
    dec, qd, kd, cd = _retention_tables()
    hw = RET_HEADS * RET_DK
    tpb, nb, steps, x_specs, out_spec = _skewed_specs(bsz, seq, tl, d)
    rope = pl.BlockSpec((tpb * tl, RET_DK), lambda s: (s % nb, 0))
    return pl.pallas_call(
        functools.partial(_ret_kernel, tl=tl, tpb=tpb, nb=nb, chunk_decay=cd),
        grid=(steps,),
        in_specs=[
            *x_specs,
            _const_spec((1, d)),
            _const_spec((d, 4 * hw + d)),
            rope, rope,
            _const_spec((RET_HEADS, CHUNK, CHUNK)),
            _const_spec((CHUNK, hw)),
            _const_spec((hw, CHUNK)),
            _const_spec((1, hw)),
            _const_spec((hw, d)),
            _const_spec((1, d)),
        ],
        out_specs=out_spec,
        out_shape=jax.ShapeDtypeStruct((bsz, seq, d), BF16),
        scratch_shapes=[
            pltpu.VMEM((tl, 4 * hw + d), F32),
            pltpu.VMEM((tl, 4 * hw + d), F32),
            pltpu.VMEM((2 * tl, hw), BF16),
            pltpu.VMEM((RET_HEADS, RET_DK, RET_DK), F32),
        ],
        compiler_params=_mixer_params(),
        name="retention",
    )(x, x, norm_w, w_r, cos2, sin2, dec, qd, kd, gn_w, w_o, merge_b)


def _ssd_kernel(x_ref, xn_ref, nw_ref, w_ref, cw_ref, cb_ref, dtb_ref, alog_ref, dsk_ref, ex_ref,
                gnw_ref, wo_ref, mb_ref, o_ref, pa_ref, pb_ref, hx_ref, hb_ref, xc_ref, y_ref, st_ref, *,
                tl, tpb, nb):
    gw = SSD_INNER // SSD_GROUPS
    hpg = SSD_HEADS // SSD_GROUPS
    xbc = slice(SSD_INNER, SSD_INNER + SSD_CONV_DIM)

    def inproj(x, p_ref):
        p_ref[...] = _mm(_rmsnorm(x, nw_ref[...]).astype(BF16), w_ref[...])

    def shift_rows(v, n, hist):
        r = pltpu.roll(v, n, 0)
        sub = lax.broadcasted_iota(jnp.int32, (SUBLANE, v.shape[1]), 0)
        head = jnp.where(sub < n, pltpu.roll(hist, n, 0), r[0:SUBLANE])
        return jnp.concatenate([head, r[SUBLANE:]], axis=0)

    def step(p_ref, slot, xn, pn_ref, out0):
        base = slot * tl
        hn = _rmsnorm(xn, nw_ref[...]).astype(BF16)
        emit, _ = _slab_emitter(hn, w_ref, pn_ref)
        lane = lax.broadcasted_iota(jnp.int32, (CHUNK, LANE), 1)

        def expand(v):
            hi, mid, lo = _split3(v)
            packed = jnp.where(lane < SSD_HEADS, hi, jnp.where(lane < 2 * SSD_HEADS, mid, lo))
            return _mm(packed, ex_ref[...])

        def conv(rows):
            xin = p_ref[rows, xbc]
            x1 = shift_rows(xin, 1, hx_ref[...])
            hx_ref[...] = xin[CHUNK - SUBLANE:, :]
            near = cb_ref[...] + cw_ref[3:4, :] * xin + cw_ref[2:3, :] * x1
            far = cw_ref[1:2, :] * xin + cw_ref[0:1, :] * x1
            xc_ref[rows, :] = _silu(near + shift_rows(far, 2, hb_ref[...]))
            hb_ref[...] = far[CHUNK - SUBLANE:, :]

        dt = _softplus(p_ref[:, SSD_INNER + SSD_CONV_DIM:SSD_INNER + SSD_CONV_DIM + LANE]
                       + dtb_ref[...])
        a = dt * (-jnp.exp(alog_ref[...]))
        causal, tril = _tril_bf16(CHUNK)
        zero_b = jnp.zeros((CHUNK, LANE), BF16)
        nchunk = tl // CHUNK
        for c in range(nchunk):
            rows = slice(c * CHUNK, (c + 1) * CHUNK)
            orow = slice(base + c * CHUNK, base + (c + 1) * CHUNK)
            conv(rows)
            emit(1)
            a_cs = _mm_sel(tril, a[rows])
            dt_t = dt[rows].T
            emit(1)
            a_last = a_cs[CHUNK - 1:CHUNK, :]
            a2 = a_cs * LOG2E
            a2t = a2.T
            ecs = expand(jnp.exp(a_cs))
            dend = expand(dt[rows] * jnp.exp(a_last - a_cs))
            emit(1)
            sx = xc_ref[rows, 0:SSD_INNER]
            xs_b = sx.astype(BF16)
            xd_b = (sx * dend).astype(BF16)
            for g in range(SSD_GROUPS):
                gc = slice(g * gw, (g + 1) * gw)
                bmat_t = xc_ref[rows, SSD_INNER + g * SSD_N: SSD_INNER + (g + 1) * SSD_N].T.astype(BF16)
                cmat = xc_ref[rows, SSD_INNER + (SSD_GROUPS + g) * SSD_N:
                              SSD_INNER + (SSD_GROUPS + g + 1) * SSD_N].astype(BF16)
                cb = _mm(cmat, bmat_t)
                state = st_ref[:, gc]
                y_off = _mm(cmat, state.astype(BF16))
                upd = _mm(bmat_t, xd_b[:, gc])
                emit(1)
                cb = jnp.where(causal, cb, 0.0)
                pieces = []
                for r in range(0, hpg, 2):
                    hh = g * hpg + r
                    pair = xs_b[:, hh * SSD_P:(hh + 2) * SSD_P]
                    rhs = jnp.concatenate([jnp.where(lane < SSD_P, pair, zero_b),
                                           jnp.where(lane >= SSD_P, pair, zero_b)], axis=0)
                    lhs = []
                    for h2 in (hh, hh + 1):
                        seg = jnp.minimum(a2[:, h2:h2 + 1] - a2t[h2:h2 + 1, :], 0.0)
                        lhs.append((cb * jnp.exp2(seg) * dt_t[h2:h2 + 1, :]).astype(BF16))
                    pieces.append(_mm(jnp.concatenate(lhs, axis=1), rhs))
                y_diag = jnp.concatenate(pieces, axis=1)
                y = y_diag + y_off * ecs[:, gc] + sx[:, gc] * dsk_ref[:, gc]
                st_ref[:, gc] = ecs[CHUNK - 1:CHUNK, gc] * state + upd
                y = y * _silu(p_ref[rows, gc])
                y = y * lax.rsqrt(jnp.mean(y * y, axis=-1, keepdims=True) + EPS) * gnw_ref[:, gc]
                y_ref[orow, gc] = y.astype(BF16)
                emit(1 if c < nchunk - 1 or g < SSD_GROUPS - 1 else None)
        o_ref[0, pl.ds(out0, tl), :] = _gated_out(y_ref[base:base + tl, :], wo_ref, p_ref, mb_ref,
                                                  SSD_INNER + SSD_CONV_DIM + LANE).astype(o_ref.dtype)

    def reset():
        st_ref[...] = jnp.zeros_like(st_ref)
        hx_ref[...] = jnp.zeros_like(hx_ref)
        hb_ref[...] = jnp.zeros_like(hb_ref)

    _run_skewed(step, inproj, reset, x_ref, xn_ref, pa_ref, pb_ref, tl=tl, tpb=tpb, nb=nb)


def _ssd_call(x, norm_w, w_s, conv_w, conv_b, dt_bias, a_log, d_skip, gn_w, w_o, merge_b, tl):
    bsz, seq, d = x.shape
    wcols = SSD_INNER + SSD_CONV_DIM + LANE + d
    pad = LANE - SSD_REP * SSD_HEADS
    dtb = jnp.pad(jnp.tile(dt_bias, SSD_REP), (0, pad)).reshape(1, LANE)
    alog = jnp.pad(jnp.tile(a_log, SSD_REP), (0, pad)).reshape(1, LANE)
    dsk = jnp.repeat(d_skip, SSD_P).reshape(1, SSD_INNER)
    expand = np.zeros((LANE, SSD_INNER), np.float32)
    for r in range(SSD_REP * SSD_HEADS):
        hh = r % SSD_HEADS
        expand[r, hh * SSD_P:(hh + 1) * SSD_P] = 1.0
    expand = jnp.asarray(expand, BF16)
    tpb, nb, steps, x_specs, out_spec = _skewed_specs(bsz, seq, tl, d)
    return pl.pallas_call(
        functools.partial(_ssd_kernel, tl=tl, tpb=tpb, nb=nb),
        grid=(steps,),
        in_specs=[
            *x_specs,
            _const_spec((1, d)),
            _const_spec((d, wcols)),
            _const_spec((SSD_CONV, SSD_CONV_DIM)),
            _const_spec((1, SSD_CONV_DIM)),
            _const_spec((1, LANE)),
            _const_spec((1, LANE)),
            _const_spec((1, SSD_INNER)),
            _const_spec((LANE, SSD_INNER)),
            _const_spec((1, SSD_INNER)),
            _const_spec((SSD_INNER, d)),
            _const_spec((1, d)),
        ],
        out_specs=out_spec,
        out_shape=jax.ShapeDtypeStruct((bsz, seq, d), BF16),
        scratch_shapes=[
            pltpu.VMEM((tl, wcols), F32),
            pltpu.VMEM((tl, wcols), F32),
            pltpu.VMEM((SUBLANE, SSD_CONV_DIM), F32),
            pltpu.VMEM((SUBLANE, SSD_CONV_DIM), F32),
            pltpu.VMEM((tl, SSD_CONV_DIM), F32),
            pltpu.VMEM((2 * tl, SSD_INNER), BF16),
            pltpu.VMEM((SSD_N, SSD_INNER), F32),
        ],
        compiler_params=_mixer_params(),
        name="ssd",
    )(x, x, norm_w, w_s, conv_w, conv_b.reshape(1, -1), dtb, alog, dsk, expand, gn_w, w_o, merge_b)


def _gla_kernel(x_ref, xn_ref, nw_ref, w_ref, gw_ref, gb_ref, gnw_ref, wo_ref, mb_ref,
                o_ref, pa_ref, pb_ref, g_ref, st_ref, *, tl, tpb, nb):
    qk = GLA_HEADS * GLA_DK
    vw = GLA_HEADS * GLA_DV
    nsub = CHUNK // GLA_SUB
    scale = GLA_DK ** -0.5

    lr0 = 2 * qk + 2 * vw

    def inproj(x, p_ref):
        p_ref[...] = _mm(_rmsnorm(x, nw_ref[...]).astype(BF16), w_ref[...])

    def step(p_ref, slot, xn, pn_ref, out0):
        base = slot * tl
        hn = _rmsnorm(xn, nw_ref[...]).astype(BF16)
        emit, _ = _slab_emitter(hn, w_ref, pn_ref)
        z = _mm(p_ref[:, lr0:lr0 + LANE].astype(BF16), gw_ref[...]) + gb_ref[...]
        emit(2)
        la = -_softplus(-z) * (LOG2E / GLA_TAU)
        causal, tril = _tril_bf16(CHUNK)
        lane_s = lax.broadcasted_iota(jnp.int32, (qk, CHUNK), 1)
        nchunk = tl // CHUNK
        bcs_all = [_mm_sel(tril, la[c * CHUNK:(c + 1) * CHUNK]) for c in range(nchunk)]
        emit(3)
        work = []
        for c in range(nchunk):
            rows = slice(c * CHUNK, (c + 1) * CHUNK)
            bcs = bcs_all[c]
            bcs_t = bcs.T
            k_t = p_ref[rows, qk:2 * qk].T
            eprev = [jnp.zeros((1, qk), F32)] + [bcs[GLA_SUB * j - 1:GLA_SUB * j, :] for j in range(1, nsub)]
            eprev_x = jnp.concatenate([jnp.broadcast_to(e, (GLA_SUB, qk)) for e in eprev], axis=0)
            q_in = p_ref[rows, 0:qk] * scale * jnp.exp2(bcs - eprev_x)
            q_in_b = q_in.astype(BF16)
            qb = (q_in * jnp.exp2(eprev_x)).astype(BF16)
            e_last_t = bcs_t[:, CHUNK - 1:CHUNK]
            kend_t = (k_t * jnp.exp2(e_last_t - bcs_t)).astype(BF16)
            work.append((rows, bcs_t, k_t, q_in_b, qb, e_last_t, kend_t))
        srows_all = []
        for rows, bcs_t, k_t, q_in_b, qb, e_last_t, kend_t in work:
            srows = [[] for _ in range(GLA_HEADS)]
            for j in range(nsub):
                start = bcs_t[:, GLA_SUB * j - 1:GLA_SUB * j] if j else 0.0
                arg = jnp.where(lane_s < GLA_SUB * (j + 1), start - bcs_t, -jnp.inf)
                khat_t = (k_t * jnp.exp2(arg)).astype(BF16)
                for hd in range(GLA_HEADS):
                    dk = slice(hd * GLA_DK, (hd + 1) * GLA_DK)
                    srows[hd].append(_mm(q_in_b[GLA_SUB * j:GLA_SUB * (j + 1), dk], khat_t[dk, :]))
                if j == nsub // 2 - 1:
                    emit(1)
            srows_all.append(srows)
        for c, (rows, bcs_t, k_t, q_in_b, qb, e_last_t, kend_t) in enumerate(work):
            orow = slice(base + c * CHUNK, base + (c + 1) * CHUNK)
            decay = jnp.exp2(e_last_t)
            for hd in range(GLA_HEADS):
                cols = slice(hd * GLA_DV, (hd + 1) * GLA_DV)
                dk = slice(hd * GLA_DK, (hd + 1) * GLA_DK)
                scores = jnp.where(causal, jnp.concatenate(srows_all[c][hd], axis=0), 0.0).astype(BF16)
                v = p_ref[rows, 2 * qk + hd * GLA_DV: 2 * qk + (hd + 1) * GLA_DV].astype(BF16)
                state = st_ref[dk, :]
                o = _mm(scores, v) + _mm(qb[:, dk], state.astype(BF16))
                st_ref[dk, :] = decay[dk, :] * state + _mm(kend_t[dk, :], v)
                o = o * lax.rsqrt(jnp.mean(o * o, axis=-1, keepdims=True) + EPS) * gnw_ref[:, cols]
                gate = p_ref[rows, 2 * qk + vw + hd * GLA_DV: 2 * qk + vw + (hd + 1) * GLA_DV]
                g_ref[orow, cols] = (_silu(gate) * o).astype(BF16)
            emit(2 if c < nchunk - 1 else None)
        o_ref[0, pl.ds(out0, tl), :] = _gated_out(g_ref[base:base + tl, :], wo_ref, p_ref, mb_ref,
                                                  lr0 + LANE).astype(o_ref.dtype)

    def reset():
        st_ref[...] = jnp.zeros_like(st_ref)

    _run_skewed(step, inproj, reset, x_ref, xn_ref, pa_ref, pb_ref, tl=tl, tpb=tpb, nb=nb)


def _gla_call(x, norm_w, w_g, gate_w, gate_b, gn_w, w_o, merge_b, tl):
    bsz, seq, d = x.shape
    qk = GLA_HEADS * GLA_DK
    vw = GLA_HEADS * GLA_DV
    wcols = 2 * qk + 2 * vw + LANE + d
    gw = jnp.pad(gate_w, ((0, LANE - GLA_RANK), (0, 0))).astype(BF16)
    tpb, nb, steps, x_specs, out_spec = _skewed_specs(bsz, seq, tl, d)
    return pl.pallas_call(
        functools.partial(_gla_kernel, tl=tl, tpb=tpb, nb=nb),
        grid=(steps,),
        in_specs=[
            *x_specs,
            _const_spec((1, d)),
            _const_spec((d, wcols)),
            _const_spec((LANE, qk)),
            _const_spec((1, qk)),
            _const_spec((1, vw)),
            _const_spec((vw, d)),
            _const_spec((1, d)),
        ],
        out_specs=out_spec,
        out_shape=jax.ShapeDtypeStruct((bsz, seq, d), BF16),
        scratch_shapes=[
            pltpu.VMEM((tl, wcols), F32),
            pltpu.VMEM((tl, wcols), F32),
            pltpu.VMEM((2 * tl, vw), BF16),
            pltpu.VMEM((qk, GLA_DV), F32),
        ],
        compiler_params=_mixer_params(),
        name="gla",
    )(x, x, norm_w, w_g, gw, gate_b.reshape(1, -1), gn_w, w_o, merge_b)


def _merge_mlp_kernel(x_ref, a_ref, b_ref, c_ref, wout_ref, mnw_ref, wup_ref, wdn_ref, fnw_ref, o_ref, *,
                      final_norm, tm):
    def sub_tile(i, carry):
        rows = pl.ds(pl.multiple_of(i * tm, tm), tm)
        x = x_ref[0, rows, :]
        merged = (a_ref[0, rows, :].astype(F32) + b_ref[0, rows, :].astype(F32)
                  + c_ref[0, rows, :].astype(F32))
        x1 = x + _mm(merged.astype(BF16), wout_ref[...])
        h2 = _rmsnorm(x1, mnw_ref[...]).astype(BF16)
        acc = x1
        for k in range(D_FF // FF_CHUNK):
            u = jnp.maximum(_mm(h2, wup_ref[:, k * FF_CHUNK:(k + 1) * FF_CHUNK]), 0.0)
            acc = acc + _mm((u * u).astype(BF16), wdn_ref[k * FF_CHUNK:(k + 1) * FF_CHUNK, :])
        if final_norm:
            acc = _rmsnorm(acc, fnw_ref[...])
        o_ref[0, rows, :] = acc
        return carry

    lax.fori_loop(0, x_ref.shape[1] // tm, sub_tile, 0)


def _merge_mlp_call(x, ret_o, ssd_o, gla_o, w_out, mlp_norm_w, w_up, w_down, final_norm_w, final_norm, tm):
    bsz, seq, d = x.shape
    tl = math.gcd(seq, MLP_BLOCK)
    tok = pl.BlockSpec((1, tl, d), lambda b, t: (b, t, 0))
    const = lambda shape: pl.BlockSpec(shape, lambda b, t: (0,) * len(shape))
    return pl.pallas_call(
        functools.partial(_merge_mlp_kernel, final_norm=final_norm, tm=min(tm, tl)),
        grid=(bsz, seq // tl),
        in_specs=[
            tok, tok, tok, tok,
            const((d, d)),
            const((1, d)),
            const((d, D_FF)),
            const((D_FF, d)),
            const((1, d)),
        ],
        out_specs=tok,
        out_shape=jax.ShapeDtypeStruct((bsz, seq, d), F32),
        compiler_params=pltpu.CompilerParams(
            dimension_semantics=("arbitrary", "arbitrary"), vmem_limit_bytes=VMEM_LIMIT),
        name="merge_mlp",
    )(x, ret_o, ssd_o, gla_o, w_out, mlp_norm_w, w_up, w_down, final_norm_w)


def _pad_cols(w, width):
    return jnp.pad(w, ((0, 0), (0, width - w.shape[1])))


def kernel(x, attn_norm_w, w_in, ret_norm_w, ret_w_o, ssd_conv_w, ssd_conv_b, ssd_dt_bias, ssd_a_log, ssd_d,
           ssd_norm_w, ssd_w_o, gla_gate_w, gla_gate_b, gla_norm_w, gla_w_o, merge_gate_b, w_out, mlp_norm_w,
           w_up, w_down, final_norm_w):
    depth = w_in.shape[0]
    seq = x.shape[1]
    d = x.shape[2]
    tl = min(MIXER_TILE, seq // 2)
    tm = min(MLP_TILE, seq)
    inv_freq = ROPE_BASE ** (-jnp.arange(0, RET_DK, 2, dtype=F32) / RET_DK)
    ang = jnp.arange(seq, dtype=F32)[:, None] * inv_freq[None, :]
    cos = jnp.cos(ang)
    sin = jnp.sin(ang)
    cos2 = jnp.concatenate([cos, cos], axis=1)
    sin2 = jnp.concatenate([-sin, sin], axis=1)

    o_ret = 0
    o_sz = 4 * RET_HEADS * RET_DK
    o_sdt = o_sz + SSD_INNER + SSD_CONV_DIM
    o_gla = o_sdt + SSD_HEADS
    o_glr = o_gla + 2 * GLA_HEADS * GLA_DK + 2 * GLA_HEADS * GLA_DV
    o_mg = o_glr + GLA_RANK
    row = lambda v: v.reshape(1, -1)
    for layer in range(depth):
        wl = w_in[layer]
        mg = [wl[:, o_mg + i * d:o_mg + (i + 1) * d] for i in range(3)]
        mb = [row(merge_gate_b[layer, i * d:(i + 1) * d]) for i in range(3)]
        w_r = jnp.concatenate([wl[:, o_ret:o_sz], mg[0]], axis=1).astype(BF16)
        w_dt = jnp.tile(wl[:, o_sdt:o_gla], (1, SSD_REP))
        w_s = jnp.concatenate([wl[:, o_sz:o_sdt], _pad_cols(w_dt, LANE), mg[1]], axis=1).astype(BF16)
        w_g = jnp.concatenate([wl[:, o_gla:o_glr], _pad_cols(wl[:, o_glr:o_mg], LANE), mg[2]],
                              axis=1).astype(BF16)
        nw = row(attn_norm_w[layer])
        ret_o = _retention_call(x, nw, w_r, cos2, sin2, row(ret_norm_w[layer]),
                                ret_w_o[layer].astype(BF16), mb[0], tl)
        ssd_o = _ssd_call(x, nw, w_s, ssd_conv_w[layer], ssd_conv_b[layer], ssd_dt_bias[layer],
                          ssd_a_log[layer], ssd_d[layer], row(ssd_norm_w[layer]),
                          ssd_w_o[layer].astype(BF16), mb[1], tl)
        gla_o = _gla_call(x, nw, w_g, gla_gate_w[layer], gla_gate_b[layer], row(gla_norm_w[layer]),
                          gla_w_o[layer].astype(BF16), mb[2], tl)
        x = _merge_mlp_call(x, ret_o, ssd_o, gla_o, w_out[layer].astype(BF16), row(mlp_norm_w[layer]),
                            w_up[layer].astype(BF16), w_down[layer].astype(BF16), row(final_norm_w),
                            layer == depth - 1, tm)
    return x
```

```python
import functools
import math

import numpy as np
import jax
import jax.numpy as jnp
from jax import lax
from jax.experimental import pallas as pl
from jax.experimental.pallas import tpu as pltpu

F32 = jnp.float32
BF16 = jnp.bfloat16

D_MODEL = 1024
RET_HEADS = 4
RET_DK = 128
SSD_HEADS = 16
SSD_P = 64
SSD_GROUPS = 2
SSD_N = 128
SSD_INNER = SSD_HEADS * SSD_P
SSD_CONV = 4
SSD_CONV_DIM = SSD_INNER + 2 * SSD_GROUPS * SSD_N
GLA_HEADS = 4
GLA_DK = 64
GLA_DV = 128
GLA_RANK = 16
GLA_TAU = 16.0
GLA_SUB = 16
D_FF = 4 * D_MODEL
EPS = 1e-6
ROPE_BASE = 10000.0
LOG2E = math.log2(math.e)
CHUNK = 128
LANE = 128
SUBLANE = 8
SLAB = 256
SSD_REP = 3
MIXER_TILE = 256
TILES_PER_STEP = 8
MLP_TILE = 512
MLP_BLOCK = 1024
FF_CHUNK = 1024
VMEM_LIMIT = 56 * 1024 * 1024


def _mm(a, b):
    return lax.dot_general(a, b, (((1,), (0,)), ((), ())), preferred_element_type=F32)


def _split3(v):
    hi = v.astype(BF16)
    r1 = v - hi.astype(F32)
    mid = r1.astype(BF16)
    lo = (r1 - mid.astype(F32)).astype(BF16)
    return hi, mid, lo


def _mm_sel(sel, v):
    hi, mid, lo = _split3(v)
    return _mm(sel, hi) + _mm(sel, mid) + _mm(sel, lo)


def _rmsnorm(x, w):
    return x * lax.rsqrt(jnp.mean(x * x, axis=-1, keepdims=True) + EPS) * w


def _sigmoid(x):
    return 0.5 + 0.5 * jnp.tanh(0.5 * x)


def _silu(x):
    h = 0.5 * x
    return h + h * jnp.tanh(h)


def _softplus(x):
    return jnp.maximum(x, 0.0) + jnp.log(1.0 + jnp.exp(-jnp.abs(x)))


def _tril_bf16(n):
    r = lax.broadcasted_iota(jnp.int32, (n, n), 0)
    c = lax.broadcasted_iota(jnp.int32, (n, n), 1)
    return r >= c, (r >= c).astype(BF16)


def _const_spec(shape):
    nd = len(shape)
    return pl.BlockSpec(shape, lambda s: (0,) * nd)


def _layer_spec(shape, layer):
    nd = len(shape)
    return pl.BlockSpec((None,) + shape, lambda *_: (layer,) + (0,) * nd, pipeline_mode=pl.Buffered(1))


def _skewed_specs(bsz, seq, tl, d):
    nt = seq // tl
    tpb = math.gcd(nt, TILES_PER_STEP)
    assert seq % tl == 0 and tpb % 2 == 0
    nb = nt // tpb
    last_tile = bsz * nt - 1
    block = pl.BlockSpec((1, tpb * tl, d), lambda s: (s // nb, s % nb, 0))

    def next_tile(s):
        g = jnp.minimum((s + 1) * tpb, last_tile)
        return (g // nt, g % nt, 0)

    x_next = pl.BlockSpec((1, tl, d), next_tile)
    return tpb, nb, bsz * nb, (block, x_next), block


def _run_skewed(step, inproj, reset, x_ref, xn_ref, pa_ref, pb_ref, *, tl, tpb, nb):
    s = pl.program_id(0)

    @pl.when(s == 0)
    def _():
        inproj(x_ref[0, 0:tl, :], pa_ref)

    @pl.when(s % nb == 0)
    def _():
        reset()

    def pair(i, carry):
        k0 = 2 * i
        r1 = pl.multiple_of((k0 + 1) * tl, tl)
        step(pa_ref, 0, x_ref[0, pl.ds(r1, tl), :], pb_ref, pl.multiple_of(k0 * tl, tl))
        r2 = pl.multiple_of(jnp.minimum(k0 + 2, tpb - 1) * tl, tl)
        xn = jnp.where(i == tpb // 2 - 1, xn_ref[0], x_ref[0, pl.ds(r2, tl), :])
        step(pb_ref, 1, xn, pa_ref, pl.multiple_of((k0 + 1) * tl, tl))
        return carry

    lax.fori_loop(0, tpb // 2, pair, 0)


def _slab_emitter(hn, w_ref, pn_ref):
    wcols = w_ref.shape[1]
    slabs = [(c0, min(c0 + SLAB, wcols)) for c0 in range(0, wcols, SLAB)]

    def emit(n=None):
        for _ in range(len(slabs) if n is None else min(n, len(slabs))):
            c0, c1 = slabs.pop(0)
            pn_ref[:, c0:c1] = _mm(hn, w_ref[:, c0:c1])

    return emit, len(slabs)


def _spread(total, points):
    return [total * (i + 1) // points - total * i // points for i in range(points)]


def _gated_out(y, wo_ref, p_ref, mb_ref, col0):
    d = wo_ref.shape[1]
    return _sigmoid(p_ref[:, col0:col0 + d] + mb_ref[...]) * _mm(y, wo_ref[...])


def _mixer_params():
    return pltpu.CompilerParams(dimension_semantics=("arbitrary",), vmem_limit_bytes=VMEM_LIMIT)


def _ret_kernel(x_ref, xn_ref, nw_ref, w_ref, cos_ref, sin_ref, dec_ref, qd_ref, kd_ref, gnw_ref,
                wo_ref, mb_ref, o_ref, pa_ref, pb_ref, g_ref, st_ref, *, tl, tpb, nb, chunk_decay):
    hw = RET_HEADS * RET_DK

    def inproj(x, p_ref):
        p_ref[...] = _mm(_rmsnorm(x, nw_ref[...]).astype(BF16), w_ref[...])

    def step(p_ref, slot, xn, pn_ref, out0):
        base = slot * tl
        hn = _rmsnorm(xn, nw_ref[...]).astype(BF16)
        emit, nslab = _slab_emitter(hn, w_ref, pn_ref)
        units = [(c, hd) for c in range(tl // CHUNK) for hd in range(RET_HEADS)]
        per_unit = _spread(nslab, len(units))
        for u, (c, hd) in enumerate(units):
            rows = slice(c * CHUNK, (c + 1) * CHUNK)
            orow = slice(base + c * CHUNK, base + (c + 1) * CHUNK)
            cols = slice(hd * RET_DK, (hd + 1) * RET_DK)
            cos = cos_ref[pl.ds(out0 + c * CHUNK, CHUNK), :]
            sin = sin_ref[pl.ds(out0 + c * CHUNK, CHUNK), :]
            q = p_ref[rows, cols]
            k = p_ref[rows, hw + hd * RET_DK: hw + (hd + 1) * RET_DK]
            v = p_ref[rows, 2 * hw + hd * RET_DK: 2 * hw + (hd + 1) * RET_DK].astype(BF16)
            q = q * cos + pltpu.roll(q, RET_DK // 2, 1) * sin
            k = k * cos + pltpu.roll(k, RET_DK // 2, 1) * sin
            k_t = k.T
            scores = _mm(q.astype(BF16), k_t.astype(BF16))
            kv = _mm((k_t * kd_ref[cols, :]).astype(BF16), v)
            emit(per_unit[u])
            state = st_ref[hd]
            o = (_mm((scores * dec_ref[hd]).astype(BF16), v)
                 + _mm((q * qd_ref[:, cols]).astype(BF16), state.astype(BF16)))
            st_ref[hd] = chunk_decay[hd] * state + kv
            o = o * lax.rsqrt(jnp.mean(o * o, axis=-1, keepdims=True) + EPS) * gnw_ref[:, cols]
            gate = p_ref[rows, 3 * hw + hd * RET_DK: 3 * hw + (hd + 1) * RET_DK]
            g_ref[orow, cols] = (_silu(gate) * o).astype(BF16)
        o_ref[0, pl.ds(out0, tl), :] = _gated_out(g_ref[base:base + tl, :], wo_ref, p_ref, mb_ref,
                                                  4 * hw).astype(o_ref.dtype)

    def reset():
        st_ref[...] = jnp.zeros_like(st_ref)

    _run_skewed(step, inproj, reset, x_ref, xn_ref, pa_ref, pb_ref, tl=tl, tpb=tpb, nb=nb)


def _retention_tables():
    lg = np.log1p(-np.exp2(-5.0 - np.arange(RET_HEADS, dtype=np.float64)))
    pos = np.arange(CHUNK, dtype=np.float64)
    scale = RET_DK ** -0.5
    dist = pos[:, None] - pos[None, :]
    dec = np.where(dist >= 0, np.exp(lg[:, None, None] * np.maximum(dist, 0.0)), 0.0) * scale
    qd = np.repeat(np.exp(lg[None, :] * (pos[:, None] + 1.0)), RET_DK, axis=1)
    kd = np.repeat(np.exp(lg[:, None] * (CHUNK - 1.0 - pos[None, :])), RET_DK, axis=0) * scale
    cd = tuple(float(v) for v in np.exp(lg * CHUNK))
    return (jnp.asarray(dec, F32), jnp.asarray(qd, F32), jnp.asarray(kd, F32), cd)


def _retention_call(x, norm_w, w_r, cos2, sin2, gn_w, w_o, merge_b, tl, layer):
    bsz, seq, d = x.shape
    dec, qd, kd, cd = _retention_tables()
    hw = RET_HEADS * RET_DK
    tpb, nb, steps, x_specs, out_spec = _skewed_specs(bsz, seq, tl, d)
    rope = pl.BlockSpec((tpb * tl, RET_DK), lambda s: (s % nb, 0))
    return pl.pallas_call(
        functools.partial(_ret_kernel, tl=tl, tpb=tpb, nb=nb, chunk_decay=cd),
        grid=(steps,),
        in_specs=[
            *x_specs,
            _const_spec((1, d)),
            _layer_spec((d, 4 * hw + d), layer),
            rope, rope,
            _const_spec((RET_HEADS, CHUNK, CHUNK)),
            _const_spec((CHUNK, hw)),
            _const_spec((hw, CHUNK)),
            _const_spec((1, hw)),
            _layer_spec((hw, d), layer),
            _const_spec((1, d)),
        ],
        out_specs=out_spec,
        out_shape=jax.ShapeDtypeStruct((bsz, seq, d), BF16),
        scratch_shapes=[
            pltpu.VMEM((tl, 4 * hw + d), F32),
            pltpu.VMEM((tl, 4 * hw + d), F32),
            pltpu.VMEM((2 * tl, hw), BF16),
            pltpu.VMEM((RET_HEADS, RET_DK, RET_DK), F32),
        ],
        compiler_params=_mixer_params(),
        name="retention",
    )(x, x, norm_w, w_r, cos2, sin2, dec, qd, kd, gn_w, w_o, merge_b)


def _ssd_kernel(x_ref, xn_ref, nw_ref, w_ref, cw_ref, cb_ref, dtb_ref, alog_ref, dsk_ref, ex_ref,
                gnw_ref, wo_ref, mb_ref, o_ref, pa_ref, pb_ref, hx_ref, hb_ref, xc_ref, y_ref, st_ref, *,
                tl, tpb, nb):
    gw = SSD_INNER // SSD_GROUPS
    hpg = SSD_HEADS // SSD_GROUPS
    xbc = slice(SSD_INNER, SSD_INNER + SSD_CONV_DIM)

    def inproj(x, p_ref):
        p_ref[...] = _mm(_rmsnorm(x, nw_ref[...]).astype(BF16), w_ref[...])

    def shift_rows(v, n, hist):
        r = pltpu.roll(v, n, 0)
        sub = lax.broadcasted_iota(jnp.int32, (SUBLANE, v.shape[1]), 0)
        head = jnp.where(sub < n, pltpu.roll(hist, n, 0), r[0:SUBLANE])
        return jnp.concatenate([head, r[SUBLANE:]], axis=0)

    def step(p_ref, slot, xn, pn_ref, out0):
        base = slot * tl
        hn = _rmsnorm(xn, nw_ref[...]).astype(BF16)
        emit, _ = _slab_emitter(hn, w_ref, pn_ref)
        lane = lax.broadcasted_iota(jnp.int32, (CHUNK, LANE), 1)

        def expand(v):
            hi, mid, lo = _split3(v)
            packed = jnp.where(lane < SSD_HEADS, hi, jnp.where(lane < 2 * SSD_HEADS, mid, lo))
            return _mm(packed, ex_ref[...])

        def conv(rows):
            xin = p_ref[rows, xbc]
            x1 = shift_rows(xin, 1, hx_ref[...])
            hx_ref[...] = xin[CHUNK - SUBLANE:, :]
            near = cb_ref[...] + cw_ref[3:4, :] * xin + cw_ref[2:3, :] * x1
            far = cw_ref[1:2, :] * xin + cw_ref[0:1, :] * x1
            xc_ref[rows, :] = _silu(near + shift_rows(far, 2, hb_ref[...]))
            hb_ref[...] = far[CHUNK - SUBLANE:, :]

        dt = _softplus(p_ref[:, SSD_INNER + SSD_CONV_DIM:SSD_INNER + SSD_CONV_DIM + LANE]
                       + dtb_ref[...])
        a = dt * (-jnp.exp(alog_ref[...]))
        causal, tril = _tril_bf16(CHUNK)
        zero_b = jnp.zeros((CHUNK, LANE), BF16)
        nchunk = tl // CHUNK
        for c in range(nchunk):
            rows = slice(c * CHUNK, (c + 1) * CHUNK)
            orow = slice(base + c * CHUNK, base + (c + 1) * CHUNK)
            conv(rows)
            emit(1)
            a_cs = _mm_sel(tril, a[rows])
            dt_t = dt[rows].T
            emit(1)
            a_last = a_cs[CHUNK - 1:CHUNK, :]
            a2 = a_cs * LOG2E
            a2t = a2.T
            ecs = expand(jnp.exp(a_cs))
            dend = expand(dt[rows] * jnp.exp(a_last - a_cs))
            emit(1)
            sx = xc_ref[rows, 0:SSD_INNER]
            xs_b = sx.astype(BF16)
            xd_b = (sx * dend).astype(BF16)
            for g in range(SSD_GROUPS):
                gc = slice(g * gw, (g + 1) * gw)
                bmat_t = xc_ref[rows, SSD_INNER + g * SSD_N: SSD_INNER + (g + 1) * SSD_N].T.astype(BF16)
                cmat = xc_ref[rows, SSD_INNER + (SSD_GROUPS + g) * SSD_N:
                              SSD_INNER + (SSD_GROUPS + g + 1) * SSD_N].astype(BF16)
                cb = _mm(cmat, bmat_t)
                state = st_ref[:, gc]
                y_off = _mm(cmat, state.astype(BF16))
                upd = _mm(bmat_t, xd_b[:, gc])
                emit(1)
                cb = jnp.where(causal, cb, 0.0)
                pieces = []
                for r in range(0, hpg, 2):
                    hh = g * hpg + r
                    pair = xs_b[:, hh * SSD_P:(hh + 2) * SSD_P]
                    rhs = jnp.concatenate([jnp.where(lane < SSD_P, pair, zero_b),
                                           jnp.where(lane >= SSD_P, pair, zero_b)], axis=0)
                    lhs = []
                    for h2 in (hh, hh + 1):
                        seg = jnp.minimum(a2[:, h2:h2 + 1] - a2t[h2:h2 + 1, :], 0.0)
                        lhs.append((cb * jnp.exp2(seg) * dt_t[h2:h2 + 1, :]).astype(BF16))
                    pieces.append(_mm(jnp.concatenate(lhs, axis=1), rhs))
                y_diag = jnp.concatenate(pieces, axis=1)
                y = y_diag + y_off * ecs[:, gc] + sx[:, gc] * dsk_ref[:, gc]
                st_ref[:, gc] = ecs[CHUNK - 1:CHUNK, gc] * state + upd
                y = y * _silu(p_ref[rows, gc])
                y = y * lax.rsqrt(jnp.mean(y * y, axis=-1, keepdims=True) + EPS) * gnw_ref[:, gc]
                y_ref[orow, gc] = y.astype(BF16)
                emit(1 if c < nchunk - 1 or g < SSD_GROUPS - 1 else None)
        o_ref[0, pl.ds(out0, tl), :] = _gated_out(y_ref[base:base + tl, :], wo_ref, p_ref, mb_ref,
                                                  SSD_INNER + SSD_CONV_DIM + LANE).astype(o_ref.dtype)

    def reset():
        st_ref[...] = jnp.zeros_like(st_ref)
        hx_ref[...] = jnp.zeros_like(hx_ref)
        hb_ref[...] = jnp.zeros_like(hb_ref)

    _run_skewed(step, inproj, reset, x_ref, xn_ref, pa_ref, pb_ref, tl=tl, tpb=tpb, nb=nb)


def _ssd_call(x, norm_w, w_s, conv_w, conv_b, dt_bias, a_log, d_skip, gn_w, w_o, merge_b, tl, layer):
    bsz, seq, d = x.shape
    wcols = SSD_INNER + SSD_CONV_DIM + LANE + d
    pad = LANE - SSD_REP * SSD_HEADS
    dtb = jnp.pad(jnp.tile(dt_bias, SSD_REP), (0, pad)).reshape(1, LANE)
    alog = jnp.pad(jnp.tile(a_log, SSD_REP), (0, pad)).reshape(1, LANE)
    dsk = jnp.repeat(d_skip, SSD_P).reshape(1, SSD_INNER)
    expand = np.zeros((LANE, SSD_INNER), np.float32)
    for r in range(SSD_REP * SSD_HEADS):
        hh = r % SSD_HEADS
        expand[r, hh * SSD_P:(hh + 1) * SSD_P] = 1.0
    expand = jnp.asarray(expand, BF16)
    tpb, nb, steps, x_specs, out_spec = _skewed_specs(bsz, seq, tl, d)
    return pl.pallas_call(
        functools.partial(_ssd_kernel, tl=tl, tpb=tpb, nb=nb),
        grid=(steps,),
        in_specs=[
            *x_specs,
            _const_spec((1, d)),
            _layer_spec((d, wcols), layer),
            _const_spec((SSD_CONV, SSD_CONV_DIM)),
            _const_spec((1, SSD_CONV_DIM)),
            _const_spec((1, LANE)),
            _const_spec((1, LANE)),
            _const_spec((1, SSD_INNER)),
            _const_spec((LANE, SSD_INNER)),
            _const_spec((1, SSD_INNER)),
            _layer_spec((SSD_INNER, d), layer),
            _const_spec((1, d)),
        ],
        out_specs=out_spec,
        out_shape=jax.ShapeDtypeStruct((bsz, seq, d), BF16),
        scratch_shapes=[
            pltpu.VMEM((tl, wcols), F32),
            pltpu.VMEM((tl, wcols), F32),
            pltpu.VMEM((SUBLANE, SSD_CONV_DIM), F32),
            pltpu.VMEM((SUBLANE, SSD_CONV_DIM), F32),
            pltpu.VMEM((tl, SSD_CONV_DIM), F32),
            pltpu.VMEM((2 * tl, SSD_INNER), BF16),
            pltpu.VMEM((SSD_N, SSD_INNER), F32),
        ],
        compiler_params=_mixer_params(),
        name="ssd",
    )(x, x, norm_w, w_s, conv_w, conv_b.reshape(1, -1), dtb, alog, dsk, expand, gn_w, w_o, merge_b)


def _gla_kernel(x_ref, xn_ref, nw_ref, w_ref, gw_ref, gb_ref, gnw_ref, wo_ref, mb_ref,
                o_ref, pa_ref, pb_ref, g_ref, st_ref, *, tl, tpb, nb):
    qk = GLA_HEADS * GLA_DK
    vw = GLA_HEADS * GLA_DV
    nsub = CHUNK // GLA_SUB
    scale = GLA_DK ** -0.5

    lr0 = 2 * qk + 2 * vw

    def inproj(x, p_ref):
        p_ref[...] = _mm(_rmsnorm(x, nw_ref[...]).astype(BF16), w_ref[...])

    def step(p_ref, slot, xn, pn_ref, out0):
        base = slot * tl
        hn = _rmsnorm(xn, nw_ref[...]).astype(BF16)
        emit, _ = _slab_emitter(hn, w_ref, pn_ref)
        z = _mm(p_ref[:, lr0:lr0 + LANE].astype(BF16), gw_ref[...]) + gb_ref[...]
        emit(2)
        la = -_softplus(-z) * (LOG2E / GLA_TAU)
        causal, tril = _tril_bf16(CHUNK)
        lane_s = lax.broadcasted_iota(jnp.int32, (qk, CHUNK), 1)
        nchunk = tl // CHUNK
        bcs_all = [_mm_sel(tril, la[c * CHUNK:(c + 1) * CHUNK]) for c in range(nchunk)]
        emit(3)
        work = []
        for c in range(nchunk):
            rows = slice(c * CHUNK, (c + 1) * CHUNK)
            bcs = bcs_all[c]
            bcs_t = bcs.T
            k_t = p_ref[rows, qk:2 * qk].T
            eprev = [jnp.zeros((1, qk), F32)] + [bcs[GLA_SUB * j - 1:GLA_SUB * j, :] for j in range(1, nsub)]
            eprev_x = jnp.concatenate([jnp.broadcast_to(e, (GLA_SUB, qk)) for e in eprev], axis=0)
            q_in = p_ref[rows, 0:qk] * scale * jnp.exp2(bcs - eprev_x)
            q_in_b = q_in.astype(BF16)
            qb = (q_in * jnp.exp2(eprev_x)).astype(BF16)
            e_last_t = bcs_t[:, CHUNK - 1:CHUNK]
            kend_t = (k_t * jnp.exp2(e_last_t - bcs_t)).astype(BF16)
            work.append((rows, bcs_t, k_t, q_in_b, qb, e_last_t, kend_t))
        srows_all = []
        for rows, bcs_t, k_t, q_in_b, qb, e_last_t, kend_t in work:
            srows = [[] for _ in range(GLA_HEADS)]
            for j in range(nsub):
                start = bcs_t[:, GLA_SUB * j - 1:GLA_SUB * j] if j else 0.0
                arg = jnp.where(lane_s < GLA_SUB * (j + 1), start - bcs_t, -jnp.inf)
                khat_t = (k_t * jnp.exp2(arg)).astype(BF16)
                for hd in range(GLA_HEADS):
                    dk = slice(hd * GLA_DK, (hd + 1) * GLA_DK)
                    srows[hd].append(_mm(q_in_b[GLA_SUB * j:GLA_SUB * (j + 1), dk], khat_t[dk, :]))
                if j == nsub // 2 - 1:
                    emit(1)
            srows_all.append(srows)
        for c, (rows, bcs_t, k_t, q_in_b, qb, e_last_t, kend_t) in enumerate(work):
            orow = slice(base + c * CHUNK, base + (c + 1) * CHUNK)
            decay = jnp.exp2(e_last_t)
            for hd in range(GLA_HEADS):
                cols = slice(hd * GLA_DV, (hd + 1) * GLA_DV)
                dk = slice(hd * GLA_DK, (hd + 1) * GLA_DK)
                scores = jnp.where(causal, jnp.concatenate(srows_all[c][hd], axis=0), 0.0).astype(BF16)
                v = p_ref[rows, 2 * qk + hd * GLA_DV: 2 * qk + (hd + 1) * GLA_DV].astype(BF16)
                state = st_ref[dk, :]
                o = _mm(scores, v) + _mm(qb[:, dk], state.astype(BF16))
                st_ref[dk, :] = decay[dk, :] * state + _mm(kend_t[dk, :], v)
                o = o * lax.rsqrt(jnp.mean(o * o, axis=-1, keepdims=True) + EPS) * gnw_ref[:, cols]
                gate = p_ref[rows, 2 * qk + vw + hd * GLA_DV: 2 * qk + vw + (hd + 1) * GLA_DV]
                g_ref[orow, cols] = (_silu(gate) * o).astype(BF16)
            emit(2 if c < nchunk - 1 else None)
        o_ref[0, pl.ds(out0, tl), :] = _gated_out(g_ref[base:base + tl, :], wo_ref, p_ref, mb_ref,
                                                  lr0 + LANE).astype(o_ref.dtype)

    def reset():
        st_ref[...] = jnp.zeros_like(st_ref)

    _run_skewed(step, inproj, reset, x_ref, xn_ref, pa_ref, pb_ref, tl=tl, tpb=tpb, nb=nb)


def _gla_call(x, norm_w, w_g, gate_w, gate_b, gn_w, w_o, merge_b, tl, layer):
    bsz, seq, d = x.shape
    qk = GLA_HEADS * GLA_DK
    vw = GLA_HEADS * GLA_DV
    wcols = 2 * qk + 2 * vw + LANE + d
    gw = jnp.pad(gate_w, ((0, LANE - GLA_RANK), (0, 0))).astype(BF16)
    tpb, nb, steps, x_specs, out_spec = _skewed_specs(bsz, seq, tl, d)
    return pl.pallas_call(
        functools.partial(_gla_kernel, tl=tl, tpb=tpb, nb=nb),
        grid=(steps,),
        in_specs=[
            *x_specs,
            _const_spec((1, d)),
            _layer_spec((d, wcols), layer),
            _const_spec((LANE, qk)),
            _const_spec((1, qk)),
            _const_spec((1, vw)),
            _layer_spec((vw, d), layer),
            _const_spec((1, d)),
        ],
        out_specs=out_spec,
        out_shape=jax.ShapeDtypeStruct((bsz, seq, d), BF16),
        scratch_shapes=[
            pltpu.VMEM((tl, wcols), F32),
            pltpu.VMEM((tl, wcols), F32),
            pltpu.VMEM((2 * tl, vw), BF16),
            pltpu.VMEM((qk, GLA_DV), F32),
        ],
        compiler_params=_mixer_params(),
        name="gla",
    )(x, x, norm_w, w_g, gw, gate_b.reshape(1, -1), gn_w, w_o, merge_b)


def _merge_mlp_kernel(x_ref, a_ref, b_ref, c_ref, wout_ref, mnw_ref, wup_ref, wdn_ref, fnw_ref, o_ref, *,
                      final_norm, tm):
    def sub_tile(i, carry):
        rows = pl.ds(pl.multiple_of(i * tm, tm), tm)
        x = x_ref[0, rows, :]
        merged = (a_ref[0, rows, :].astype(F32) + b_ref[0, rows, :].astype(F32)
                  + c_ref[0, rows, :].astype(F32))
        x1 = x + _mm(merged.astype(BF16), wout_ref[...])
        h2 = _rmsnorm(x1, mnw_ref[...]).astype(BF16)
        acc = x1
        for k in range(D_FF // FF_CHUNK):
            u = jnp.maximum(_mm(h2, wup_ref[:, k * FF_CHUNK:(k + 1) * FF_CHUNK]), 0.0)
            acc = acc + _mm((u * u).astype(BF16), wdn_ref[k * FF_CHUNK:(k + 1) * FF_CHUNK, :])
        if final_norm:
            acc = _rmsnorm(acc, fnw_ref[...])
        o_ref[0, rows, :] = acc
        return carry

    lax.fori_loop(0, x_ref.shape[1] // tm, sub_tile, 0)


def _merge_mlp_call(x, ret_o, ssd_o, gla_o, w_out, mlp_norm_w, w_up, w_down, final_norm_w, final_norm, tm, layer):
    bsz, seq, d = x.shape
    tl = math.gcd(seq, MLP_BLOCK)
    tok = pl.BlockSpec((1, tl, d), lambda b, t: (b, t, 0))
    const = lambda shape: pl.BlockSpec(shape, lambda b, t: (0,) * len(shape))
    stacked = lambda shape: _layer_spec(shape, layer)
    return pl.pallas_call(
        functools.partial(_merge_mlp_kernel, final_norm=final_norm, tm=min(tm, tl)),
        grid=(bsz, seq // tl),
        in_specs=[
            tok, tok, tok, tok,
            stacked((d, d)),
            const((1, d)),
            stacked((d, D_FF)),
            stacked((D_FF, d)),
            const((1, d)),
        ],
        out_specs=tok,
        out_shape=jax.ShapeDtypeStruct((bsz, seq, d), F32),
        compiler_params=pltpu.CompilerParams(
            dimension_semantics=("arbitrary", "arbitrary"), vmem_limit_bytes=VMEM_LIMIT),
        name="merge_mlp",
    )(x, ret_o, ssd_o, gla_o, w_out, mlp_norm_w, w_up, w_down, final_norm_w)


def kernel(x, attn_norm_w, w_in, ret_norm_w, ret_w_o, ssd_conv_w, ssd_conv_b, ssd_dt_bias, ssd_a_log, ssd_d,
           ssd_norm_w, ssd_w_o, gla_gate_w, gla_gate_b, gla_norm_w, gla_w_o, merge_gate_b, w_out, mlp_norm_w,
           w_up, w_down, final_norm_w):
    depth = w_in.shape[0]
    seq = x.shape[1]
    d = x.shape[2]
    tl = min(MIXER_TILE, seq // 2)
    tm = min(MLP_TILE, seq)
    inv_freq = ROPE_BASE ** (-jnp.arange(0, RET_DK, 2, dtype=F32) / RET_DK)
    ang = jnp.arange(seq, dtype=F32)[:, None] * inv_freq[None, :]
    cos = jnp.cos(ang)
    sin = jnp.sin(ang)
    cos2 = jnp.concatenate([cos, cos], axis=1)
    sin2 = jnp.concatenate([-sin, sin], axis=1)

    o_ret = 0
    o_sz = 4 * RET_HEADS * RET_DK
    o_sdt = o_sz + SSD_INNER + SSD_CONV_DIM
    o_gla = o_sdt + SSD_HEADS
    o_glr = o_gla + 2 * GLA_HEADS * GLA_DK + 2 * GLA_HEADS * GLA_DV
    o_mg = o_glr + GLA_RANK
    row = lambda v: v.reshape(1, -1)
    pad3 = lambda w: jnp.pad(w, ((0, 0), (0, 0), (0, LANE - w.shape[2])))
    mg = [w_in[:, :, o_mg + i * d:o_mg + (i + 1) * d] for i in range(3)]
    w_r = jnp.concatenate([w_in[:, :, o_ret:o_sz], mg[0]], axis=2).astype(BF16)
    w_dt = jnp.tile(w_in[:, :, o_sdt:o_gla], (1, 1, SSD_REP))
    w_s = jnp.concatenate([w_in[:, :, o_sz:o_sdt], pad3(w_dt), mg[1]], axis=2).astype(BF16)
    w_g = jnp.concatenate([w_in[:, :, o_gla:o_glr], pad3(w_in[:, :, o_glr:o_mg]), mg[2]], axis=2).astype(BF16)
    ret_wo, ssd_wo, gla_wo = ret_w_o.astype(BF16), ssd_w_o.astype(BF16), gla_w_o.astype(BF16)
    w_out_b, w_up_b, w_down_b = w_out.astype(BF16), w_up.astype(BF16), w_down.astype(BF16)
    for layer in range(depth):
        mb = [row(merge_gate_b[layer, i * d:(i + 1) * d]) for i in range(3)]
        nw = row(attn_norm_w[layer])
        ret_o = _retention_call(x, nw, w_r, cos2, sin2, row(ret_norm_w[layer]), ret_wo, mb[0], tl, layer)
        ssd_o = _ssd_call(x, nw, w_s, ssd_conv_w[layer], ssd_conv_b[layer], ssd_dt_bias[layer],
                          ssd_a_log[layer], ssd_d[layer], row(ssd_norm_w[layer]), ssd_wo, mb[1], tl, layer)
        gla_o = _gla_call(x, nw, w_g, gla_gate_w[layer], gla_gate_b[layer], row(gla_norm_w[layer]),
                          gla_wo, mb[2], tl, layer)
        x = _merge_mlp_call(x, ret_o, ssd_o, gla_o, w_out_b, row(mlp_norm_w[layer]), w_up_b, w_down_b,
                            row(final_norm_w), layer == depth - 1, tm, layer)
    return x
```

```python
import functools
import math

import numpy as np
import jax
import jax.numpy as jnp
from jax import lax
from jax.experimental import pallas as pl
from jax.experimental.pallas import tpu as pltpu

F32 = jnp.float32
BF16 = jnp.bfloat16

D_MODEL = 1024
RET_HEADS = 4
RET_DK = 128
SSD_HEADS = 16
SSD_P = 64
SSD_GROUPS = 2
SSD_N = 128
SSD_INNER = SSD_HEADS * SSD_P
SSD_CONV = 4
SSD_CONV_DIM = SSD_INNER + 2 * SSD_GROUPS * SSD_N
GLA_HEADS = 4
GLA_DK = 64
GLA_DV = 128
GLA_RANK = 16
GLA_TAU = 16.0
GLA_SUB = 16
D_FF = 4 * D_MODEL
EPS = 1e-6
ROPE_BASE = 10000.0
LOG2E = math.log2(math.e)
CHUNK = 128
LANE = 128
SUBLANE = 8
SLAB = 256
SSD_REP = 3
MIXER_TILE = 256
TILES_PER_STEP = 8
MLP_TILE = 512
MLP_BLOCK = 1024
FF_CHUNK = 1024
REGROUP_ROWS = 128
VMEM_LIMIT = 56 * 1024 * 1024


def _mm(a, b):
    return lax.dot_general(a, b, (((1,), (0,)), ((), ())), preferred_element_type=F32)


def _split3(v):
    hi = v.astype(BF16)
    r1 = v - hi.astype(F32)
    mid = r1.astype(BF16)
    lo = (r1 - mid.astype(F32)).astype(BF16)
    return hi, mid, lo


def _mm_sel(sel, v):
    hi, mid, lo = _split3(v)
    return _mm(sel, hi) + _mm(sel, mid) + _mm(sel, lo)


def _rmsnorm(x, w):
    return x * lax.rsqrt(jnp.mean(x * x, axis=-1, keepdims=True) + EPS) * w


def _sigmoid(x):
    return 0.5 + 0.5 * jnp.tanh(0.5 * x)


def _silu(x):
    h = 0.5 * x
    return h + h * jnp.tanh(h)


def _softplus(x):
    return jnp.maximum(x, 0.0) + jnp.log(1.0 + jnp.exp(-jnp.abs(x)))


def _tril_bf16(n):
    r = lax.broadcasted_iota(jnp.int32, (n, n), 0)
    c = lax.broadcasted_iota(jnp.int32, (n, n), 1)
    return r >= c, (r >= c).astype(BF16)


def _const_spec(shape):
    nd = len(shape)
    return pl.BlockSpec(shape, lambda s: (0,) * nd)


def _layer_spec(shape, layer):
    nd = len(shape)
    return pl.BlockSpec((None,) + shape, lambda *_: (layer,) + (0,) * nd, pipeline_mode=pl.Buffered(1))


def _skewed_specs(bsz, seq, tl, d):
    nt = seq // tl
    tpb = math.gcd(nt, TILES_PER_STEP)
    assert seq % tl == 0 and tpb % 2 == 0
    nb = nt // tpb
    last_tile = bsz * nt - 1
    block = pl.BlockSpec((1, tpb * tl, d), lambda s: (s // nb, s % nb, 0))

    def next_tile(s):
        g = jnp.minimum((s + 1) * tpb, last_tile)
        return (g // nt, g % nt, 0)

    x_next = pl.BlockSpec((1, tl, d), next_tile)
    return tpb, nb, bsz * nb, (block, x_next), block


def _run_skewed(step, inproj, reset, x_ref, xn_ref, pa_ref, pb_ref, *, tl, tpb, nb):
    s = pl.program_id(0)

    @pl.when(s == 0)
    def _():
        inproj(x_ref[0, 0:tl, :], pa_ref)

    @pl.when(s % nb == 0)
    def _():
        reset()

    def pair(i, carry):
        k0 = 2 * i
        r1 = pl.multiple_of((k0 + 1) * tl, tl)
        step(pa_ref, 0, x_ref[0, pl.ds(r1, tl), :], pb_ref, pl.multiple_of(k0 * tl, tl))
        r2 = pl.multiple_of(jnp.minimum(k0 + 2, tpb - 1) * tl, tl)
        xn = jnp.where(i == tpb // 2 - 1, xn_ref[0], x_ref[0, pl.ds(r2, tl), :])
        step(pb_ref, 1, xn, pa_ref, pl.multiple_of((k0 + 1) * tl, tl))
        return carry

    lax.fori_loop(0, tpb // 2, pair, 0)


def _slab_emitter(hn, w_ref, pn_ref):
    wcols = w_ref.shape[1]
    slabs = [(c0, min(c0 + SLAB, wcols)) for c0 in range(0, wcols, SLAB)]

    def emit(n=None):
        for _ in range(len(slabs) if n is None else min(n, len(slabs))):
            c0, c1 = slabs.pop(0)
            pn_ref[:, c0:c1] = _mm(hn, w_ref[:, c0:c1])

    return emit, len(slabs)


def _spread(total, points):
    return [total * (i + 1) // points - total * i // points for i in range(points)]


def _gated_out(y, wo_ref, p_ref, mb_ref, col0):
    d = wo_ref.shape[1]
    return _sigmoid(p_ref[:, col0:col0 + d] + mb_ref[...]) * _mm(y, wo_ref[...])


def _mixer_params():
    return pltpu.CompilerParams(dimension_semantics=("arbitrary",), vmem_limit_bytes=VMEM_LIMIT)


def _ret_kernel(x_ref, xn_ref, nw_ref, w_ref, cos_ref, sin_ref, dec_ref, qd_ref, kd_ref, gnw_ref,
                wo_ref, mb_ref, o_ref, pa_ref, pb_ref, g_ref, st_ref, *, tl, tpb, nb, chunk_decay):
    hw = RET_HEADS * RET_DK

    def inproj(x, p_ref):
        p_ref[...] = _mm(_rmsnorm(x, nw_ref[...]).astype(BF16), w_ref[...])

    def step(p_ref, slot, xn, pn_ref, out0):
        base = slot * tl
        hn = _rmsnorm(xn, nw_ref[...]).astype(BF16)
        emit, nslab = _slab_emitter(hn, w_ref, pn_ref)
        units = [(c, hd) for c in range(tl // CHUNK) for hd in range(RET_HEADS)]
        per_unit = _spread(nslab, len(units))
        for u, (c, hd) in enumerate(units):
            rows = slice(c * CHUNK, (c + 1) * CHUNK)
            orow = slice(base + c * CHUNK, base + (c + 1) * CHUNK)
            cols = slice(hd * RET_DK, (hd + 1) * RET_DK)
            cos = cos_ref[pl.ds(out0 + c * CHUNK, CHUNK), :]
            sin = sin_ref[pl.ds(out0 + c * CHUNK, CHUNK), :]
            q = p_ref[rows, cols]
            k = p_ref[rows, hw + hd * RET_DK: hw + (hd + 1) * RET_DK]
            v = p_ref[rows, 2 * hw + hd * RET_DK: 2 * hw + (hd + 1) * RET_DK].astype(BF16)
            q = q * cos + pltpu.roll(q, RET_DK // 2, 1) * sin
            k = k * cos + pltpu.roll(k, RET_DK // 2, 1) * sin
            k_t = k.T
            scores = _mm(q.astype(BF16), k_t.astype(BF16))
            kv = _mm((k_t * kd_ref[cols, :]).astype(BF16), v)
            emit(per_unit[u])
            state = st_ref[hd]
            o = (_mm((scores * dec_ref[hd]).astype(BF16), v)
                 + _mm((q * qd_ref[:, cols]).astype(BF16), state.astype(BF16)))
            st_ref[hd] = chunk_decay[hd] * state + kv
            o = o * lax.rsqrt(jnp.mean(o * o, axis=-1, keepdims=True) + EPS) * gnw_ref[:, cols]
            gate = p_ref[rows, 3 * hw + hd * RET_DK: 3 * hw + (hd + 1) * RET_DK]
            g_ref[orow, cols] = (_silu(gate) * o).astype(BF16)
        o_ref[0, pl.ds(out0, tl), :] = _gated_out(g_ref[base:base + tl, :], wo_ref, p_ref, mb_ref,
                                                  4 * hw).astype(o_ref.dtype)

    def reset():
        st_ref[...] = jnp.zeros_like(st_ref)

    _run_skewed(step, inproj, reset, x_ref, xn_ref, pa_ref, pb_ref, tl=tl, tpb=tpb, nb=nb)


def _retention_tables():
    lg = np.log1p(-np.exp2(-5.0 - np.arange(RET_HEADS, dtype=np.float64)))
    pos = np.arange(CHUNK, dtype=np.float64)
    scale = RET_DK ** -0.5
    dist = pos[:, None] - pos[None, :]
    dec = np.where(dist >= 0, np.exp(lg[:, None, None] * np.maximum(dist, 0.0)), 0.0) * scale
    qd = np.repeat(np.exp(lg[None, :] * (pos[:, None] + 1.0)), RET_DK, axis=1)
    kd = np.repeat(np.exp(lg[:, None] * (CHUNK - 1.0 - pos[None, :])), RET_DK, axis=0) * scale
    cd = tuple(float(v) for v in np.exp(lg * CHUNK))
    return (jnp.asarray(dec, F32), jnp.asarray(qd, F32), jnp.asarray(kd, F32), cd)


def _retention_call(x, norm_w, w_r, cos2, sin2, gn_w, w_o, merge_b, tl, layer):
    bsz, seq, d = x.shape
    dec, qd, kd, cd = _retention_tables()
    hw = RET_HEADS * RET_DK
    tpb, nb, steps, x_specs, out_spec = _skewed_specs(bsz, seq, tl, d)
    rope = pl.BlockSpec((tpb * tl, RET_DK), lambda s: (s % nb, 0))
    return pl.pallas_call(
        functools.partial(_ret_kernel, tl=tl, tpb=tpb, nb=nb, chunk_decay=cd),
        grid=(steps,),
        in_specs=[
            *x_specs,
            _const_spec((1, d)),
            _layer_spec((d, 4 * hw + d), layer),
            rope, rope,
            _const_spec((RET_HEADS, CHUNK, CHUNK)),
            _const_spec((CHUNK, hw)),
            _const_spec((hw, CHUNK)),
            _const_spec((1, hw)),
            _layer_spec((hw, d), layer),
            _const_spec((1, d)),
        ],
        out_specs=out_spec,
        out_shape=jax.ShapeDtypeStruct((bsz, seq, d), BF16),
        scratch_shapes=[
            pltpu.VMEM((tl, 4 * hw + d), F32),
            pltpu.VMEM((tl, 4 * hw + d), F32),
            pltpu.VMEM((2 * tl, hw), BF16),
            pltpu.VMEM((RET_HEADS, RET_DK, RET_DK), F32),
        ],
        compiler_params=_mixer_params(),
        name="retention",
    )(x, x, norm_w, w_r, cos2, sin2, dec, qd, kd, gn_w, w_o, merge_b)


def _ssd_kernel(x_ref, xn_ref, nw_ref, w_ref, cw_ref, cb_ref, dtb_ref, alog_ref, dsk_ref, ex_ref,
                gnw_ref, wo_ref, mb_ref, o_ref, pa_ref, pb_ref, hx_ref, hb_ref, xc_ref, y_ref, st_ref, *,
                tl, tpb, nb):
    gw = SSD_INNER // SSD_GROUPS
    hpg = SSD_HEADS // SSD_GROUPS
    xbc = slice(SSD_INNER, SSD_INNER + SSD_CONV_DIM)

    def inproj(x, p_ref):
        p_ref[...] = _mm(_rmsnorm(x, nw_ref[...]).astype(BF16), w_ref[...])

    def shift_rows(v, n, hist):
        r = pltpu.roll(v, n, 0)
        sub = lax.broadcasted_iota(jnp.int32, (SUBLANE, v.shape[1]), 0)
        head = jnp.where(sub < n, pltpu.roll(hist, n, 0), r[0:SUBLANE])
        return jnp.concatenate([head, r[SUBLANE:]], axis=0)

    def step(p_ref, slot, xn, pn_ref, out0):
        base = slot * tl
        hn = _rmsnorm(xn, nw_ref[...]).astype(BF16)
        emit, _ = _slab_emitter(hn, w_ref, pn_ref)
        lane = lax.broadcasted_iota(jnp.int32, (CHUNK, LANE), 1)

        def expand(v):
            hi, mid, lo = _split3(v)
            packed = jnp.where(lane < SSD_HEADS, hi, jnp.where(lane < 2 * SSD_HEADS, mid, lo))
            return _mm(packed, ex_ref[...])

        def conv(rows):
            xin = p_ref[rows, xbc]
            x1 = shift_rows(xin, 1, hx_ref[...])
            hx_ref[...] = xin[CHUNK - SUBLANE:, :]
            near = cb_ref[...] + cw_ref[3:4, :] * xin + cw_ref[2:3, :] * x1
            far = cw_ref[1:2, :] * xin + cw_ref[0:1, :] * x1
            xc_ref[rows, :] = _silu(near + shift_rows(far, 2, hb_ref[...]))
            hb_ref[...] = far[CHUNK - SUBLANE:, :]

        dt = _softplus(p_ref[:, SSD_INNER + SSD_CONV_DIM:SSD_INNER + SSD_CONV_DIM + LANE]
                       + dtb_ref[...])
        a = dt * (-jnp.exp(alog_ref[...]))
        causal, tril = _tril_bf16(CHUNK)
        zero_b = jnp.zeros((CHUNK, LANE), BF16)
        nchunk = tl // CHUNK
        for c in range(nchunk):
            rows = slice(c * CHUNK, (c + 1) * CHUNK)
            orow = slice(base + c * CHUNK, base + (c + 1) * CHUNK)
            conv(rows)
            emit(1)
            a_cs = _mm_sel(tril, a[rows])
            dt_t = dt[rows].T
            emit(1)
            a_last = a_cs[CHUNK - 1:CHUNK, :]
            a2 = a_cs * LOG2E
            a2t = a2.T
            ecs = expand(jnp.exp(a_cs))
            dend = expand(dt[rows] * jnp.exp(a_last - a_cs))
            emit(1)
            sx = xc_ref[rows, 0:SSD_INNER]
            xs_b = sx.astype(BF16)
            xd_b = (sx * dend).astype(BF16)
            for g in range(SSD_GROUPS):
                gc = slice(g * gw, (g + 1) * gw)
                bmat_t = xc_ref[rows, SSD_INNER + g * SSD_N: SSD_INNER + (g + 1) * SSD_N].T.astype(BF16)
                cmat = xc_ref[rows, SSD_INNER + (SSD_GROUPS + g) * SSD_N:
                              SSD_INNER + (SSD_GROUPS + g + 1) * SSD_N].astype(BF16)
                cb = _mm(cmat, bmat_t)
                state = st_ref[:, gc]
                y_off = _mm(cmat, state.astype(BF16))
                upd = _mm(bmat_t, xd_b[:, gc])
                emit(1)
                cb = jnp.where(causal, cb, 0.0)
                pieces = []
                for r in range(0, hpg, 2):
                    hh = g * hpg + r
                    pair = xs_b[:, hh * SSD_P:(hh + 2) * SSD_P]
                    rhs = jnp.concatenate([jnp.where(lane < SSD_P, pair, zero_b),
                                           jnp.where(lane >= SSD_P, pair, zero_b)], axis=0)
                    lhs = []
                    for h2 in (hh, hh + 1):
                        seg = jnp.minimum(a2[:, h2:h2 + 1] - a2t[h2:h2 + 1, :], 0.0)
                        lhs.append((cb * jnp.exp2(seg) * dt_t[h2:h2 + 1, :]).astype(BF16))
                    pieces.append(_mm(jnp.concatenate(lhs, axis=1), rhs))
                y_diag = jnp.concatenate(pieces, axis=1)
                y = y_diag + y_off * ecs[:, gc] + sx[:, gc] * dsk_ref[:, gc]
                st_ref[:, gc] = ecs[CHUNK - 1:CHUNK, gc] * state + upd
                y = y * _silu(p_ref[rows, gc])
                y = y * lax.rsqrt(jnp.mean(y * y, axis=-1, keepdims=True) + EPS) * gnw_ref[:, gc]
                y_ref[orow, gc] = y.astype(BF16)
                emit(1 if c < nchunk - 1 or g < SSD_GROUPS - 1 else None)
        o_ref[0, pl.ds(out0, tl), :] = _gated_out(y_ref[base:base + tl, :], wo_ref, p_ref, mb_ref,
                                                  SSD_INNER + SSD_CONV_DIM + LANE).astype(o_ref.dtype)

    def reset():
        st_ref[...] = jnp.zeros_like(st_ref)
        hx_ref[...] = jnp.zeros_like(hx_ref)
        hb_ref[...] = jnp.zeros_like(hb_ref)

    _run_skewed(step, inproj, reset, x_ref, xn_ref, pa_ref, pb_ref, tl=tl, tpb=tpb, nb=nb)


def _ssd_call(x, norm_w, w_s, conv_w, conv_b, dt_bias, a_log, d_skip, gn_w, w_o, merge_b, tl, layer):
    bsz, seq, d = x.shape
    wcols = SSD_INNER + SSD_CONV_DIM + LANE + d
    pad = LANE - SSD_REP * SSD_HEADS
    dtb = jnp.pad(jnp.tile(dt_bias, SSD_REP), (0, pad)).reshape(1, LANE)
    alog = jnp.pad(jnp.tile(a_log, SSD_REP), (0, pad)).reshape(1, LANE)
    dsk = jnp.repeat(d_skip, SSD_P).reshape(1, SSD_INNER)
    expand = np.zeros((LANE, SSD_INNER), np.float32)
    for r in range(SSD_REP * SSD_HEADS):
        hh = r % SSD_HEADS
        expand[r, hh * SSD_P:(hh + 1) * SSD_P] = 1.0
    expand = jnp.asarray(expand, BF16)
    tpb, nb, steps, x_specs, out_spec = _skewed_specs(bsz, seq, tl, d)
    return pl.pallas_call(
        functools.partial(_ssd_kernel, tl=tl, tpb=tpb, nb=nb),
        grid=(steps,),
        in_specs=[
            *x_specs,
            _const_spec((1, d)),
            _layer_spec((d, wcols), layer),
            _const_spec((SSD_CONV, SSD_CONV_DIM)),
            _const_spec((1, SSD_CONV_DIM)),
            _const_spec((1, LANE)),
            _const_spec((1, LANE)),
            _const_spec((1, SSD_INNER)),
            _const_spec((LANE, SSD_INNER)),
            _const_spec((1, SSD_INNER)),
            _layer_spec((SSD_INNER, d), layer),
            _const_spec((1, d)),
        ],
        out_specs=out_spec,
        out_shape=jax.ShapeDtypeStruct((bsz, seq, d), BF16),
        scratch_shapes=[
            pltpu.VMEM((tl, wcols), F32),
            pltpu.VMEM((tl, wcols), F32),
            pltpu.VMEM((SUBLANE, SSD_CONV_DIM), F32),
            pltpu.VMEM((SUBLANE, SSD_CONV_DIM), F32),
            pltpu.VMEM((tl, SSD_CONV_DIM), F32),
            pltpu.VMEM((2 * tl, SSD_INNER), BF16),
            pltpu.VMEM((SSD_N, SSD_INNER), F32),
        ],
        compiler_params=_mixer_params(),
        name="ssd",
    )(x, x, norm_w, w_s, conv_w, conv_b.reshape(1, -1), dtb, alog, dsk, expand, gn_w, w_o, merge_b)


def _gla_kernel(x_ref, xn_ref, nw_ref, w_ref, gw_ref, gb_ref, gnw_ref, wo_ref, mb_ref,
                o_ref, pa_ref, pb_ref, g_ref, st_ref, *, tl, tpb, nb):
    qk = GLA_HEADS * GLA_DK
    vw = GLA_HEADS * GLA_DV
    nsub = CHUNK // GLA_SUB
    scale = GLA_DK ** -0.5

    lr0 = 2 * qk + 2 * vw

    def inproj(x, p_ref):
        p_ref[...] = _mm(_rmsnorm(x, nw_ref[...]).astype(BF16), w_ref[...])

    def step(p_ref, slot, xn, pn_ref, out0):
        base = slot * tl
        hn = _rmsnorm(xn, nw_ref[...]).astype(BF16)
        emit, _ = _slab_emitter(hn, w_ref, pn_ref)
        z = _mm(p_ref[:, lr0:lr0 + LANE].astype(BF16), gw_ref[...]) + gb_ref[...]
        emit(2)
        la = -_softplus(-z) * (LOG2E / GLA_TAU)
        causal, tril = _tril_bf16(CHUNK)
        lane_s = lax.broadcasted_iota(jnp.int32, (qk, CHUNK), 1)
        nchunk = tl // CHUNK
        bcs_all = [_mm_sel(tril, la[c * CHUNK:(c + 1) * CHUNK]) for c in range(nchunk)]
        emit(3)
        work = []
        for c in range(nchunk):
            rows = slice(c * CHUNK, (c + 1) * CHUNK)
            bcs = bcs_all[c]
            bcs_t = bcs.T
            k_t = p_ref[rows, qk:2 * qk].T
            eprev = [jnp.zeros((1, qk), F32)] + [bcs[GLA_SUB * j - 1:GLA_SUB * j, :] for j in range(1, nsub)]
            eprev_x = jnp.concatenate([jnp.broadcast_to(e, (GLA_SUB, qk)) for e in eprev], axis=0)
            q_in = p_ref[rows, 0:qk] * scale * jnp.exp2(bcs - eprev_x)
            q_in_b = q_in.astype(BF16)
            qb = (q_in * jnp.exp2(eprev_x)).astype(BF16)
            e_last_t = bcs_t[:, CHUNK - 1:CHUNK]
            kend_t = (k_t * jnp.exp2(e_last_t - bcs_t)).astype(BF16)
            work.append((rows, bcs_t, k_t, q_in_b, qb, e_last_t, kend_t))
        srows_all = []
        for rows, bcs_t, k_t, q_in_b, qb, e_last_t, kend_t in work:
            srows = [[] for _ in range(GLA_HEADS)]
            for j in range(nsub):
                start = bcs_t[:, GLA_SUB * j - 1:GLA_SUB * j] if j else 0.0
                arg = jnp.where(lane_s < GLA_SUB * (j + 1), start - bcs_t, -jnp.inf)
                khat_t = (k_t * jnp.exp2(arg)).astype(BF16)
                for hd in range(GLA_HEADS):
                    dk = slice(hd * GLA_DK, (hd + 1) * GLA_DK)
                    srows[hd].append(_mm(q_in_b[GLA_SUB * j:GLA_SUB * (j + 1), dk], khat_t[dk, :]))
                if j == nsub // 2 - 1:
                    emit(1)
            srows_all.append(srows)
        for c, (rows, bcs_t, k_t, q_in_b, qb, e_last_t, kend_t) in enumerate(work):
            orow = slice(base + c * CHUNK, base + (c + 1) * CHUNK)
            decay = jnp.exp2(e_last_t)
            for hd in range(GLA_HEADS):
                cols = slice(hd * GLA_DV, (hd + 1) * GLA_DV)
                dk = slice(hd * GLA_DK, (hd + 1) * GLA_DK)
                scores = jnp.where(causal, jnp.concatenate(srows_all[c][hd], axis=0), 0.0).astype(BF16)
                v = p_ref[rows, 2 * qk + hd * GLA_DV: 2 * qk + (hd + 1) * GLA_DV].astype(BF16)
                state = st_ref[dk, :]
                o = _mm(scores, v) + _mm(qb[:, dk], state.astype(BF16))
                st_ref[dk, :] = decay[dk, :] * state + _mm(kend_t[dk, :], v)
                o = o * lax.rsqrt(jnp.mean(o * o, axis=-1, keepdims=True) + EPS) * gnw_ref[:, cols]
                gate = p_ref[rows, 2 * qk + vw + hd * GLA_DV: 2 * qk + vw + (hd + 1) * GLA_DV]
                g_ref[orow, cols] = (_silu(gate) * o).astype(BF16)
            emit(2 if c < nchunk - 1 else None)
        o_ref[0, pl.ds(out0, tl), :] = _gated_out(g_ref[base:base + tl, :], wo_ref, p_ref, mb_ref,
                                                  lr0 + LANE).astype(o_ref.dtype)

    def reset():
        st_ref[...] = jnp.zeros_like(st_ref)

    _run_skewed(step, inproj, reset, x_ref, xn_ref, pa_ref, pb_ref, tl=tl, tpb=tpb, nb=nb)


def _gla_call(x, norm_w, w_g, gate_w, gate_b, gn_w, w_o, merge_b, tl, layer):
    bsz, seq, d = x.shape
    qk = GLA_HEADS * GLA_DK
    vw = GLA_HEADS * GLA_DV
    wcols = 2 * qk + 2 * vw + LANE + d
    gw = jnp.pad(gate_w, ((0, LANE - GLA_RANK), (0, 0))).astype(BF16)
    tpb, nb, steps, x_specs, out_spec = _skewed_specs(bsz, seq, tl, d)
    return pl.pallas_call(
        functools.partial(_gla_kernel, tl=tl, tpb=tpb, nb=nb),
        grid=(steps,),
        in_specs=[
            *x_specs,
            _const_spec((1, d)),
            _layer_spec((d, wcols), layer),
            _const_spec((LANE, qk)),
            _const_spec((1, qk)),
            _const_spec((1, vw)),
            _layer_spec((vw, d), layer),
            _const_spec((1, d)),
        ],
        out_specs=out_spec,
        out_shape=jax.ShapeDtypeStruct((bsz, seq, d), BF16),
        scratch_shapes=[
            pltpu.VMEM((tl, wcols), F32),
            pltpu.VMEM((tl, wcols), F32),
            pltpu.VMEM((2 * tl, vw), BF16),
            pltpu.VMEM((qk, GLA_DV), F32),
        ],
        compiler_params=_mixer_params(),
        name="gla",
    )(x, x, norm_w, w_g, gw, gate_b.reshape(1, -1), gn_w, w_o, merge_b)


def _merge_mlp_kernel(x_ref, a_ref, b_ref, c_ref, wout_ref, mnw_ref, wup_ref, wdn_ref, fnw_ref, o_ref, *,
                      final_norm, tm):
    def sub_tile(i, carry):
        rows = pl.ds(pl.multiple_of(i * tm, tm), tm)
        x = x_ref[0, rows, :]
        merged = (a_ref[0, rows, :].astype(F32) + b_ref[0, rows, :].astype(F32)
                  + c_ref[0, rows, :].astype(F32))
        x1 = x + _mm(merged.astype(BF16), wout_ref[...])
        h2 = _rmsnorm(x1, mnw_ref[...]).astype(BF16)
        acc = x1
        for k in range(D_FF // FF_CHUNK):
            u = jnp.maximum(_mm(h2, wup_ref[:, k * FF_CHUNK:(k + 1) * FF_CHUNK]), 0.0)
            acc = acc + _mm((u * u).astype(BF16), wdn_ref[k * FF_CHUNK:(k + 1) * FF_CHUNK, :])
        if final_norm:
            acc = _rmsnorm(acc, fnw_ref[...])
        o_ref[0, rows, :] = acc
        return carry

    lax.fori_loop(0, x_ref.shape[1] // tm, sub_tile, 0)


def _merge_mlp_call(x, ret_o, ssd_o, gla_o, w_out, mlp_norm_w, w_up, w_down, final_norm_w, final_norm, tm, layer):
    bsz, seq, d = x.shape
    tl = math.gcd(seq, MLP_BLOCK)
    tok = pl.BlockSpec((1, tl, d), lambda b, t: (b, t, 0))
    const = lambda shape: pl.BlockSpec(shape, lambda b, t: (0,) * len(shape))
    stacked = lambda shape: _layer_spec(shape, layer)
    return pl.pallas_call(
        functools.partial(_merge_mlp_kernel, final_norm=final_norm, tm=min(tm, tl)),
        grid=(bsz, seq // tl),
        in_specs=[
            tok, tok, tok, tok,
            stacked((d, d)),
            const((1, d)),
            stacked((d, D_FF)),
            stacked((D_FF, d)),
            const((1, d)),
        ],
        out_specs=tok,
        out_shape=jax.ShapeDtypeStruct((bsz, seq, d), F32),
        compiler_params=pltpu.CompilerParams(
            dimension_semantics=("arbitrary", "arbitrary"), vmem_limit_bytes=VMEM_LIMIT),
        name="merge_mlp",
    )(x, ret_o, ssd_o, gla_o, w_out, mlp_norm_w, w_up, w_down, final_norm_w)


def _regroup_kernel(w_ref, r_ref, s_ref, g_ref, *, offs):
    o_ret, o_sz, o_sdt, o_gla, o_glr, o_mg, d = offs
    rows = w_ref.shape[1]
    cols = lambda a, b: w_ref[0, :, a:b].astype(BF16)
    r_ref[0, :, 0:o_sz] = cols(o_ret, o_sz)
    r_ref[0, :, o_sz:o_sz + d] = cols(o_mg, o_mg + d)
    n_s = o_sdt - o_sz
    s_ref[0, :, 0:n_s] = cols(o_sz, o_sdt)
    dt = cols(o_sdt, o_gla)
    s_ref[0, :, n_s:n_s + LANE] = jnp.concatenate(
        [dt] * SSD_REP + [jnp.zeros((rows, LANE - SSD_REP * SSD_HEADS), BF16)], axis=1)
    s_ref[0, :, n_s + LANE:n_s + LANE + d] = cols(o_mg + d, o_mg + 2 * d)
    n_g = o_glr - o_gla
    g_ref[0, :, 0:n_g] = cols(o_gla, o_glr)
    g_ref[0, :, n_g:n_g + LANE] = jnp.concatenate(
        [cols(o_glr, o_mg), jnp.zeros((rows, LANE - GLA_RANK), BF16)], axis=1)
    g_ref[0, :, n_g + LANE:n_g + LANE + d] = cols(o_mg + 2 * d, o_mg + 3 * d)


def _regroup_call(w_in, offs):
    depth, d_in, width = w_in.shape
    o_ret, o_sz, o_sdt, o_gla, o_glr, o_mg, d = offs
    widths = (o_sz - o_ret + d, o_sdt - o_sz + LANE + d, o_glr - o_gla + LANE + d)
    tr = REGROUP_ROWS
    return pl.pallas_call(
        functools.partial(_regroup_kernel, offs=offs),
        grid=(depth, d_in // tr),
        in_specs=[pl.BlockSpec((1, tr, width), lambda l, i: (l, i, 0))],
        out_specs=[pl.BlockSpec((1, tr, wd), lambda l, i: (l, i, 0)) for wd in widths],
        out_shape=[jax.ShapeDtypeStruct((depth, d_in, wd), BF16) for wd in widths],
        compiler_params=pltpu.CompilerParams(
            dimension_semantics=("arbitrary", "arbitrary"), vmem_limit_bytes=VMEM_LIMIT),
        name="regroup",
    )(w_in)


def kernel(x, attn_norm_w, w_in, ret_norm_w, ret_w_o, ssd_conv_w, ssd_conv_b, ssd_dt_bias, ssd_a_log, ssd_d,
           ssd_norm_w, ssd_w_o, gla_gate_w, gla_gate_b, gla_norm_w, gla_w_o, merge_gate_b, w_out, mlp_norm_w,
           w_up, w_down, final_norm_w):
    depth = w_in.shape[0]
    seq = x.shape[1]
    d = x.shape[2]
    tl = min(MIXER_TILE, seq // 2)
    tm = min(MLP_TILE, seq)
    inv_freq = ROPE_BASE ** (-jnp.arange(0, RET_DK, 2, dtype=F32) / RET_DK)
    ang = jnp.arange(seq, dtype=F32)[:, None] * inv_freq[None, :]
    cos = jnp.cos(ang)
    sin = jnp.sin(ang)
    cos2 = jnp.concatenate([cos, cos], axis=1)
    sin2 = jnp.concatenate([-sin, sin], axis=1)

    o_ret = 0
    o_sz = 4 * RET_HEADS * RET_DK
    o_sdt = o_sz + SSD_INNER + SSD_CONV_DIM
    o_gla = o_sdt + SSD_HEADS
    o_glr = o_gla + 2 * GLA_HEADS * GLA_DK + 2 * GLA_HEADS * GLA_DV
    o_mg = o_glr + GLA_RANK
    row = lambda v: v.reshape(1, -1)
    w_r, w_s, w_g = _regroup_call(w_in, (o_ret, o_sz, o_sdt, o_gla, o_glr, o_mg, d))
    ret_wo, ssd_wo, gla_wo = ret_w_o.astype(BF16), ssd_w_o.astype(BF16), gla_w_o.astype(BF16)
    w_out_b, w_up_b, w_down_b = w_out.astype(BF16), w_up.astype(BF16), w_down.astype(BF16)
    for layer in range(depth):
        mb = [row(merge_gate_b[layer, i * d:(i + 1) * d]) for i in range(3)]
        nw = row(attn_norm_w[layer])
        ret_o = _retention_call(x, nw, w_r, cos2, sin2, row(ret_norm_w[layer]), ret_wo, mb[0], tl, layer)
        ssd_o = _ssd_call(x, nw, w_s, ssd_conv_w[layer], ssd_conv_b[layer], ssd_dt_bias[layer],
                          ssd_a_log[layer], ssd_d[layer], row(ssd_norm_w[layer]), ssd_wo, mb[1], tl, layer)
        gla_o = _gla_call(x, nw, w_g, gla_gate_w[layer], gla_gate_b[layer], row(gla_norm_w[layer]),
                          gla_wo, mb[2], tl, layer)
        x = _merge_mlp_call(x, ret_o, ssd_o, gla_o, w_out_b, row(mlp_norm_w[layer]), w_up_b, w_down_b,
                            row(final_norm_w), layer == depth - 1, tm, layer)
    return x
```

```python
import functools
import math

import numpy as np
import jax
import jax.numpy as jnp
from jax import lax
from jax.experimental import pallas as pl
from jax.experimental.pallas import tpu as pltpu

F32 = jnp.float32
BF16 = jnp.bfloat16

D_MODEL = 1024
RET_HEADS = 4
RET_DK = 128
SSD_HEADS = 16
SSD_P = 64
SSD_GROUPS = 2
SSD_N = 128
SSD_INNER = SSD_HEADS * SSD_P
SSD_CONV = 4
SSD_CONV_DIM = SSD_INNER + 2 * SSD_GROUPS * SSD_N
GLA_HEADS = 4
GLA_DK = 64
GLA_DV = 128
GLA_RANK = 16
GLA_TAU = 16.0
GLA_SUB = 16
D_FF = 4 * D_MODEL
EPS = 1e-6
ROPE_BASE = 10000.0
LOG2E = math.log2(math.e)
CHUNK = 128
LANE = 128
SUBLANE = 8
SLAB = 256
SSD_REP = 3
MIXER_TILE = 256
TILES_PER_STEP = 8
MLP_TILE = 512
MLP_BLOCK = 1024
FF_CHUNK = 1024
REGROUP_ROWS = 128
VMEM_LIMIT = 56 * 1024 * 1024


def _mm(a, b):
    return lax.dot_general(a, b, (((1,), (0,)), ((), ())), preferred_element_type=F32)


def _split3(v):
    hi = v.astype(BF16)
    r1 = v - hi.astype(F32)
    mid = r1.astype(BF16)
    lo = (r1 - mid.astype(F32)).astype(BF16)
    return hi, mid, lo


def _mm_sel(sel, v):
    hi, mid, lo = _split3(v)
    return _mm(sel, hi) + _mm(sel, mid) + _mm(sel, lo)


def _rmsnorm(x, w):
    return x * lax.rsqrt(jnp.mean(x * x, axis=-1, keepdims=True) + EPS) * w


def _sigmoid(x):
    return 0.5 + 0.5 * jnp.tanh(0.5 * x)


def _silu(x):
    h = 0.5 * x
    return h + h * jnp.tanh(h)


def _softplus(x):
    return jnp.maximum(x, 0.0) + jnp.log(1.0 + jnp.exp(-jnp.abs(x)))


def _tril_bf16(n):
    r = lax.broadcasted_iota(jnp.int32, (n, n), 0)
    c = lax.broadcasted_iota(jnp.int32, (n, n), 1)
    return r >= c, (r >= c).astype(BF16)


def _const_spec(shape):
    nd = len(shape)
    return pl.BlockSpec(shape, lambda s: (0,) * nd)


def _layer_spec(shape, layer):
    nd = len(shape)
    return pl.BlockSpec((None,) + shape, lambda *_: (layer,) + (0,) * nd, pipeline_mode=pl.Buffered(1))


def _skewed_specs(bsz, seq, tl, d):
    nt = seq // tl
    tpb = math.gcd(nt, TILES_PER_STEP)
    assert seq % tl == 0 and tpb % 2 == 0
    nb = nt // tpb
    last_tile = bsz * nt - 1
    block = pl.BlockSpec((1, tpb * tl, d), lambda s: (s // nb, s % nb, 0))

    def next_tile(s):
        g = jnp.minimum((s + 1) * tpb, last_tile)
        return (g // nt, g % nt, 0)

    x_next = pl.BlockSpec((1, tl, d), next_tile)
    return tpb, nb, bsz * nb, (block, x_next), block


def _run_skewed(step, inproj, reset, x_ref, xn_ref, pa_ref, pb_ref, *, tl, tpb, nb):
    s = pl.program_id(0)

    @pl.when(s == 0)
    def _():
        inproj(x_ref[0, 0:tl, :], pa_ref)

    @pl.when(s % nb == 0)
    def _():
        reset()

    def pair(i, carry):
        k0 = 2 * i
        r1 = pl.multiple_of((k0 + 1) * tl, tl)
        step(pa_ref, 0, x_ref[0, pl.ds(r1, tl), :], pb_ref, pl.multiple_of(k0 * tl, tl))
        r2 = pl.multiple_of(jnp.minimum(k0 + 2, tpb - 1) * tl, tl)
        xn = jnp.where(i == tpb // 2 - 1, xn_ref[0], x_ref[0, pl.ds(r2, tl), :])
        step(pb_ref, 1, xn, pa_ref, pl.multiple_of((k0 + 1) * tl, tl))
        return carry

    lax.fori_loop(0, tpb // 2, pair, 0)


def _slab_emitter(hn, w_ref, pn_ref):
    wcols = w_ref.shape[1]
    slabs = [(c0, min(c0 + SLAB, wcols)) for c0 in range(0, wcols, SLAB)]

    def emit(n=None):
        for _ in range(len(slabs) if n is None else min(n, len(slabs))):
            c0, c1 = slabs.pop(0)
            pn_ref[:, c0:c1] = _mm(hn, w_ref[:, c0:c1])

    return emit, len(slabs)


def _spread(total, points):
    return [total * (i + 1) // points - total * i // points for i in range(points)]


def _gated_out(y, wo_ref, p_ref, mb_ref, col0):
    d = wo_ref.shape[1]
    return _sigmoid(p_ref[:, col0:col0 + d] + mb_ref[...]) * _mm(y, wo_ref[...])


def _mixer_params():
    return pltpu.CompilerParams(dimension_semantics=("arbitrary",), vmem_limit_bytes=VMEM_LIMIT)


def _ret_kernel(x_ref, xn_ref, nw_ref, w_ref, cos_ref, sin_ref, dec_ref, qd_ref, kd_ref, gnw_ref,
                wo_ref, mb_ref, o_ref, pa_ref, pb_ref, g_ref, st_ref, *, tl, tpb, nb, chunk_decay):
    hw = RET_HEADS * RET_DK

    def inproj(x, p_ref):
        p_ref[...] = _mm(_rmsnorm(x, nw_ref[...]).astype(BF16), w_ref[...])

    def step(p_ref, slot, xn, pn_ref, out0):
        base = slot * tl
        hn = _rmsnorm(xn, nw_ref[...]).astype(BF16)
        emit, nslab = _slab_emitter(hn, w_ref, pn_ref)
        units = [(c, hd) for c in range(tl // CHUNK) for hd in range(RET_HEADS)]
        per_unit = _spread(nslab, len(units))
        for u, (c, hd) in enumerate(units):
            rows = slice(c * CHUNK, (c + 1) * CHUNK)
            orow = slice(base + c * CHUNK, base + (c + 1) * CHUNK)
            cols = slice(hd * RET_DK, (hd + 1) * RET_DK)
            cos = cos_ref[pl.ds(out0 + c * CHUNK, CHUNK), :]
            sin = sin_ref[pl.ds(out0 + c * CHUNK, CHUNK), :]
            q = p_ref[rows, cols]
            k = p_ref[rows, hw + hd * RET_DK: hw + (hd + 1) * RET_DK]
            v = p_ref[rows, 2 * hw + hd * RET_DK: 2 * hw + (hd + 1) * RET_DK].astype(BF16)
            q = q * cos + pltpu.roll(q, RET_DK // 2, 1) * sin
            k = k * cos + pltpu.roll(k, RET_DK // 2, 1) * sin
            k_t = k.T
            scores = _mm(q.astype(BF16), k_t.astype(BF16))
            kv = _mm((k_t * kd_ref[cols, :]).astype(BF16), v)
            emit(per_unit[u])
            state = st_ref[hd]
            o = (_mm((scores * dec_ref[hd]).astype(BF16), v)
                 + _mm((q * qd_ref[:, cols]).astype(BF16), state.astype(BF16)))
            st_ref[hd] = chunk_decay[hd] * state + kv
            o = o * lax.rsqrt(jnp.mean(o * o, axis=-1, keepdims=True) + EPS) * gnw_ref[:, cols]
            gate = p_ref[rows, 3 * hw + hd * RET_DK: 3 * hw + (hd + 1) * RET_DK]
            g_ref[orow, cols] = (_silu(gate) * o).astype(BF16)
        o_ref[0, pl.ds(out0, tl), :] = _gated_out(g_ref[base:base + tl, :], wo_ref, p_ref, mb_ref,
                                                  4 * hw).astype(o_ref.dtype)

    def reset():
        st_ref[...] = jnp.zeros_like(st_ref)

    _run_skewed(step, inproj, reset, x_ref, xn_ref, pa_ref, pb_ref, tl=tl, tpb=tpb, nb=nb)


def _retention_tables():
    lg = np.log1p(-np.exp2(-5.0 - np.arange(RET_HEADS, dtype=np.float64)))
    pos = np.arange(CHUNK, dtype=np.float64)
    scale = RET_DK ** -0.5
    dist = pos[:, None] - pos[None, :]
    dec = np.where(dist >= 0, np.exp(lg[:, None, None] * np.maximum(dist, 0.0)), 0.0) * scale
    qd = np.repeat(np.exp(lg[None, :] * (pos[:, None] + 1.0)), RET_DK, axis=1)
    kd = np.repeat(np.exp(lg[:, None] * (CHUNK - 1.0 - pos[None, :])), RET_DK, axis=0) * scale
    cd = tuple(float(v) for v in np.exp(lg * CHUNK))
    return (jnp.asarray(dec, F32), jnp.asarray(qd, F32), jnp.asarray(kd, F32), cd)


def _retention_call(x, norm_w, w_r, cos2, sin2, gn_w, w_o, merge_b, tl, layer):
    bsz, seq, d = x.shape
    dec, qd, kd, cd = _retention_tables()
    hw = RET_HEADS * RET_DK
    tpb, nb, steps, x_specs, out_spec = _skewed_specs(bsz, seq, tl, d)
    rope = pl.BlockSpec((tpb * tl, RET_DK), lambda s: (s % nb, 0))
    return pl.pallas_call(
        functools.partial(_ret_kernel, tl=tl, tpb=tpb, nb=nb, chunk_decay=cd),
        grid=(steps,),
        in_specs=[
            *x_specs,
            _const_spec((1, d)),
            _layer_spec((d, 4 * hw + d), layer),
            rope, rope,
            _const_spec((RET_HEADS, CHUNK, CHUNK)),
            _const_spec((CHUNK, hw)),
            _const_spec((hw, CHUNK)),
            _const_spec((1, hw)),
            _layer_spec((hw, d), layer),
            _const_spec((1, d)),
        ],
        out_specs=out_spec,
        out_shape=jax.ShapeDtypeStruct((bsz, seq, d), BF16),
        scratch_shapes=[
            pltpu.VMEM((tl, 4 * hw + d), F32),
            pltpu.VMEM((tl, 4 * hw + d), F32),
            pltpu.VMEM((2 * tl, hw), BF16),
            pltpu.VMEM((RET_HEADS, RET_DK, RET_DK), F32),
        ],
        compiler_params=_mixer_params(),
        name="retention",
    )(x, x, norm_w, w_r, cos2, sin2, dec, qd, kd, gn_w, w_o, merge_b)


def _ssd_kernel(x_ref, xn_ref, nw_ref, w_ref, cw_ref, cb_ref, dtb_ref, alog_ref, dsk_ref, ex_ref,
                gnw_ref, wo_ref, mb_ref, o_ref, pa_ref, pb_ref, hx_ref, hb_ref, xc_ref, y_ref, st_ref, *,
                tl, tpb, nb):
    gw = SSD_INNER // SSD_GROUPS
    hpg = SSD_HEADS // SSD_GROUPS
    xbc = slice(SSD_INNER, SSD_INNER + SSD_CONV_DIM)

    def inproj(x, p_ref):
        p_ref[...] = _mm(_rmsnorm(x, nw_ref[...]).astype(BF16), w_ref[...])

    def shift_rows(v, n, hist):
        r = pltpu.roll(v, n, 0)
        sub = lax.broadcasted_iota(jnp.int32, (SUBLANE, v.shape[1]), 0)
        head = jnp.where(sub < n, pltpu.roll(hist, n, 0), r[0:SUBLANE])
        return jnp.concatenate([head, r[SUBLANE:]], axis=0)

    def step(p_ref, slot, xn, pn_ref, out0):
        base = slot * tl
        hn = _rmsnorm(xn, nw_ref[...]).astype(BF16)
        emit, _ = _slab_emitter(hn, w_ref, pn_ref)
        lane = lax.broadcasted_iota(jnp.int32, (CHUNK, LANE), 1)

        def expand(v):
            hi, mid, lo = _split3(v)
            packed = jnp.where(lane < SSD_HEADS, hi, jnp.where(lane < 2 * SSD_HEADS, mid, lo))
            return _mm(packed, ex_ref[...])

        def conv(rows):
            xin = p_ref[rows, xbc]
            x1 = shift_rows(xin, 1, hx_ref[...])
            hx_ref[...] = xin[CHUNK - SUBLANE:, :]
            near = cb_ref[...] + cw_ref[3:4, :] * xin + cw_ref[2:3, :] * x1
            far = cw_ref[1:2, :] * xin + cw_ref[0:1, :] * x1
            xc_ref[rows, :] = _silu(near + shift_rows(far, 2, hb_ref[...]))
            hb_ref[...] = far[CHUNK - SUBLANE:, :]

        dt = _softplus(p_ref[:, SSD_INNER + SSD_CONV_DIM:SSD_INNER + SSD_CONV_DIM + LANE]
                       + dtb_ref[...])
        a = dt * (-jnp.exp(alog_ref[...]))
        causal, tril = _tril_bf16(CHUNK)
        zero_b = jnp.zeros((CHUNK, LANE), BF16)
        nchunk = tl // CHUNK
        for c in range(nchunk):
            rows = slice(c * CHUNK, (c + 1) * CHUNK)
            orow = slice(base + c * CHUNK, base + (c + 1) * CHUNK)
            conv(rows)
            emit(1)
            a_cs = _mm_sel(tril, a[rows])
            dt_t = dt[rows].T
            emit(1)
            a_last = a_cs[CHUNK - 1:CHUNK, :]
            a2 = a_cs * LOG2E
            a2t = a2.T
            ecs = expand(jnp.exp(a_cs))
            dend = expand(dt[rows] * jnp.exp(a_last - a_cs))
            emit(1)
            sx = xc_ref[rows, 0:SSD_INNER]
            xs_b = sx.astype(BF16)
            xd_b = (sx * dend).astype(BF16)
            for g in range(SSD_GROUPS):
                gc = slice(g * gw, (g + 1) * gw)
                bmat_t = xc_ref[rows, SSD_INNER + g * SSD_N: SSD_INNER + (g + 1) * SSD_N].T.astype(BF16)
                cmat = xc_ref[rows, SSD_INNER + (SSD_GROUPS + g) * SSD_N:
                              SSD_INNER + (SSD_GROUPS + g + 1) * SSD_N].astype(BF16)
                cb = _mm(cmat, bmat_t)
                state = st_ref[:, gc]
                y_off = _mm(cmat, state.astype(BF16))
                upd = _mm(bmat_t, xd_b[:, gc])
                emit(1)
                cb = jnp.where(causal, cb, 0.0)
                pieces = []
                for r in range(0, hpg, 2):
                    hh = g * hpg + r
                    pair = xs_b[:, hh * SSD_P:(hh + 2) * SSD_P]
                    rhs = jnp.concatenate([jnp.where(lane < SSD_P, pair, zero_b),
                                           jnp.where(lane >= SSD_P, pair, zero_b)], axis=0)
                    lhs = []
                    for h2 in (hh, hh + 1):
                        seg = jnp.minimum(a2[:, h2:h2 + 1] - a2t[h2:h2 + 1, :], 0.0)
                        lhs.append((cb * jnp.exp2(seg) * dt_t[h2:h2 + 1, :]).astype(BF16))
                    pieces.append(_mm(jnp.concatenate(lhs, axis=1), rhs))
                y_diag = jnp.concatenate(pieces, axis=1)
                y = y_diag + y_off * ecs[:, gc] + sx[:, gc] * dsk_ref[:, gc]
                st_ref[:, gc] = ecs[CHUNK - 1:CHUNK, gc] * state + upd
                y = y * _silu(p_ref[rows, gc])
                y = y * lax.rsqrt(jnp.mean(y * y, axis=-1, keepdims=True) + EPS) * gnw_ref[:, gc]
                y_ref[orow, gc] = y.astype(BF16)
                emit(1 if c < nchunk - 1 or g < SSD_GROUPS - 1 else None)
        o_ref[0, pl.ds(out0, tl), :] = _gated_out(y_ref[base:base + tl, :], wo_ref, p_ref, mb_ref,
                                                  SSD_INNER + SSD_CONV_DIM + LANE).astype(o_ref.dtype)

    def reset():
        st_ref[...] = jnp.zeros_like(st_ref)
        hx_ref[...] = jnp.zeros_like(hx_ref)
        hb_ref[...] = jnp.zeros_like(hb_ref)

    _run_skewed(step, inproj, reset, x_ref, xn_ref, pa_ref, pb_ref, tl=tl, tpb=tpb, nb=nb)


def _ssd_call(x, norm_w, w_s, conv_w, conv_b, dt_bias, a_log, d_skip, gn_w, w_o, merge_b, tl, layer):
    bsz, seq, d = x.shape
    wcols = SSD_INNER + SSD_CONV_DIM + LANE + d
    pad = LANE - SSD_REP * SSD_HEADS
    dtb = jnp.pad(jnp.tile(dt_bias, SSD_REP), (0, pad)).reshape(1, LANE)
    alog = jnp.pad(jnp.tile(a_log, SSD_REP), (0, pad)).reshape(1, LANE)
    dsk = jnp.repeat(d_skip, SSD_P).reshape(1, SSD_INNER)
    expand = np.zeros((LANE, SSD_INNER), np.float32)
    for r in range(SSD_REP * SSD_HEADS):
        hh = r % SSD_HEADS
        expand[r, hh * SSD_P:(hh + 1) * SSD_P] = 1.0
    expand = jnp.asarray(expand, BF16)
    tpb, nb, steps, x_specs, out_spec = _skewed_specs(bsz, seq, tl, d)
    return pl.pallas_call(
        functools.partial(_ssd_kernel, tl=tl, tpb=tpb, nb=nb),
        grid=(steps,),
        in_specs=[
            *x_specs,
            _const_spec((1, d)),
            _layer_spec((d, wcols), layer),
            _const_spec((SSD_CONV, SSD_CONV_DIM)),
            _const_spec((1, SSD_CONV_DIM)),
            _const_spec((1, LANE)),
            _const_spec((1, LANE)),
            _const_spec((1, SSD_INNER)),
            _const_spec((LANE, SSD_INNER)),
            _const_spec((1, SSD_INNER)),
            _layer_spec((SSD_INNER, d), layer),
            _const_spec((1, d)),
        ],
        out_specs=out_spec,
        out_shape=jax.ShapeDtypeStruct((bsz, seq, d), BF16),
        scratch_shapes=[
            pltpu.VMEM((tl, wcols), F32),
            pltpu.VMEM((tl, wcols), F32),
            pltpu.VMEM((SUBLANE, SSD_CONV_DIM), F32),
            pltpu.VMEM((SUBLANE, SSD_CONV_DIM), F32),
            pltpu.VMEM((tl, SSD_CONV_DIM), F32),
            pltpu.VMEM((2 * tl, SSD_INNER), BF16),
            pltpu.VMEM((SSD_N, SSD_INNER), F32),
        ],
        compiler_params=_mixer_params(),
        name="ssd",
    )(x, x, norm_w, w_s, conv_w, conv_b.reshape(1, -1), dtb, alog, dsk, expand, gn_w, w_o, merge_b)


def _gla_kernel(x_ref, xn_ref, nw_ref, w_ref, gw_ref, gb_ref, gnw_ref, wo_ref, mb_ref,
                o_ref, pa_ref, pb_ref, g_ref, st_ref, *, tl, tpb, nb):
    qk = GLA_HEADS * GLA_DK
    vw = GLA_HEADS * GLA_DV
    nsub = CHUNK // GLA_SUB
    scale = GLA_DK ** -0.5

    lr0 = 2 * qk + 2 * vw

    def inproj(x, p_ref):
        p_ref[...] = _mm(_rmsnorm(x, nw_ref[...]).astype(BF16), w_ref[...])

    def step(p_ref, slot, xn, pn_ref, out0):
        base = slot * tl
        hn = _rmsnorm(xn, nw_ref[...]).astype(BF16)
        emit, _ = _slab_emitter(hn, w_ref, pn_ref)
        z = _mm(p_ref[:, lr0:lr0 + LANE].astype(BF16), gw_ref[...]) + gb_ref[...]
        emit(2)
        la = -_softplus(-z) * (LOG2E / GLA_TAU)
        causal, tril = _tril_bf16(CHUNK)
        lane_s = lax.broadcasted_iota(jnp.int32, (qk, CHUNK), 1)
        nchunk = tl // CHUNK
        bcs_all = [_mm_sel(tril, la[c * CHUNK:(c + 1) * CHUNK]) for c in range(nchunk)]
        emit(3)
        work = []
        for c in range(nchunk):
            rows = slice(c * CHUNK, (c + 1) * CHUNK)
            bcs = bcs_all[c]
            bcs_t = bcs.T
            k_t = p_ref[rows, qk:2 * qk].T
            eprev = [jnp.zeros((1, qk), F32)] + [bcs[GLA_SUB * j - 1:GLA_SUB * j, :] for j in range(1, nsub)]
            eprev_x = jnp.concatenate([jnp.broadcast_to(e, (GLA_SUB, qk)) for e in eprev], axis=0)
            q_in = p_ref[rows, 0:qk] * scale * jnp.exp2(bcs - eprev_x)
            q_in_b = q_in.astype(BF16)
            qb = (q_in * jnp.exp2(eprev_x)).astype(BF16)
            e_last_t = bcs_t[:, CHUNK - 1:CHUNK]
            kend_t = (k_t * jnp.exp2(e_last_t - bcs_t)).astype(BF16)
            work.append((rows, bcs_t, k_t, q_in_b, qb, e_last_t, kend_t))
        srows_all = []
        for rows, bcs_t, k_t, q_in_b, qb, e_last_t, kend_t in work:
            srows = [[] for _ in range(GLA_HEADS)]
            for j in range(nsub):
                start = bcs_t[:, GLA_SUB * j - 1:GLA_SUB * j] if j else 0.0
                arg = jnp.where(lane_s < GLA_SUB * (j + 1), start - bcs_t, -jnp.inf)
                khat_t = (k_t * jnp.exp2(arg)).astype(BF16)
                for hd in range(GLA_HEADS):
                    dk = slice(hd * GLA_DK, (hd + 1) * GLA_DK)
                    srows[hd].append(_mm(q_in_b[GLA_SUB * j:GLA_SUB * (j + 1), dk], khat_t[dk, :]))
                if j == nsub // 2 - 1:
                    emit(1)
            srows_all.append(srows)
        for c, (rows, bcs_t, k_t, q_in_b, qb, e_last_t, kend_t) in enumerate(work):
            orow = slice(base + c * CHUNK, base + (c + 1) * CHUNK)
            decay = jnp.exp2(e_last_t)
            for hd in range(GLA_HEADS):
                cols = slice(hd * GLA_DV, (hd + 1) * GLA_DV)
                dk = slice(hd * GLA_DK, (hd + 1) * GLA_DK)
                scores = jnp.where(causal, jnp.concatenate(srows_all[c][hd], axis=0), 0.0).astype(BF16)
                v = p_ref[rows, 2 * qk + hd * GLA_DV: 2 * qk + (hd + 1) * GLA_DV].astype(BF16)
                state = st_ref[dk, :]
                o = _mm(scores, v) + _mm(qb[:, dk], state.astype(BF16))
                st_ref[dk, :] = decay[dk, :] * state + _mm(kend_t[dk, :], v)
                o = o * lax.rsqrt(jnp.mean(o * o, axis=-1, keepdims=True) + EPS) * gnw_ref[:, cols]
                gate = p_ref[rows, 2 * qk + vw + hd * GLA_DV: 2 * qk + vw + (hd + 1) * GLA_DV]
                g_ref[orow, cols] = (_silu(gate) * o).astype(BF16)
            emit(2 if c < nchunk - 1 else None)
        o_ref[0, pl.ds(out0, tl), :] = _gated_out(g_ref[base:base + tl, :], wo_ref, p_ref, mb_ref,
                                                  lr0 + LANE).astype(o_ref.dtype)

    def reset():
        st_ref[...] = jnp.zeros_like(st_ref)

    _run_skewed(step, inproj, reset, x_ref, xn_ref, pa_ref, pb_ref, tl=tl, tpb=tpb, nb=nb)


def _gla_call(x, norm_w, w_g, gate_w, gate_b, gn_w, w_o, merge_b, tl, layer):
    bsz, seq, d = x.shape
    qk = GLA_HEADS * GLA_DK
    vw = GLA_HEADS * GLA_DV
    wcols = 2 * qk + 2 * vw + LANE + d
    gw = jnp.pad(gate_w, ((0, LANE - GLA_RANK), (0, 0))).astype(BF16)
    tpb, nb, steps, x_specs, out_spec = _skewed_specs(bsz, seq, tl, d)
    return pl.pallas_call(
        functools.partial(_gla_kernel, tl=tl, tpb=tpb, nb=nb),
        grid=(steps,),
        in_specs=[
            *x_specs,
            _const_spec((1, d)),
            _layer_spec((d, wcols), layer),
            _const_spec((LANE, qk)),
            _const_spec((1, qk)),
            _const_spec((1, vw)),
            _layer_spec((vw, d), layer),
            _const_spec((1, d)),
        ],
        out_specs=out_spec,
        out_shape=jax.ShapeDtypeStruct((bsz, seq, d), BF16),
        scratch_shapes=[
            pltpu.VMEM((tl, wcols), F32),
            pltpu.VMEM((tl, wcols), F32),
            pltpu.VMEM((2 * tl, vw), BF16),
            pltpu.VMEM((qk, GLA_DV), F32),
        ],
        compiler_params=_mixer_params(),
        name="gla",
    )(x, x, norm_w, w_g, gw, gate_b.reshape(1, -1), gn_w, w_o, merge_b)


def _merge_mlp_kernel(x_ref, a_ref, b_ref, c_ref, wout_ref, mnw_ref, wup_ref, wdn_ref, fnw_ref, o_ref, *,
                      final_norm, tm):
    def sub_tile(i, carry):
        rows = pl.ds(pl.multiple_of(i * tm, tm), tm)
        x = x_ref[0, rows, :]
        merged = (a_ref[0, rows, :].astype(F32) + b_ref[0, rows, :].astype(F32)
                  + c_ref[0, rows, :].astype(F32))
        x1 = x + _mm(merged.astype(BF16), wout_ref[...])
        h2 = _rmsnorm(x1, mnw_ref[...]).astype(BF16)
        acc = x1
        for k in range(D_FF // FF_CHUNK):
            u = jnp.maximum(_mm(h2, wup_ref[:, k * FF_CHUNK:(k + 1) * FF_CHUNK]), 0.0)
            acc = acc + _mm((u * u).astype(BF16), wdn_ref[k * FF_CHUNK:(k + 1) * FF_CHUNK, :])
        if final_norm:
            acc = _rmsnorm(acc, fnw_ref[...])
        o_ref[0, rows, :] = acc
        return carry

    lax.fori_loop(0, x_ref.shape[1] // tm, sub_tile, 0)


def _merge_mlp_call(x, ret_o, ssd_o, gla_o, w_out, mlp_norm_w, w_up, w_down, final_norm_w, final_norm, tm, layer):
    bsz, seq, d = x.shape
    tl = math.gcd(seq, MLP_BLOCK)
    tok = pl.BlockSpec((1, tl, d), lambda b, t: (b, t, 0))
    const = lambda shape: pl.BlockSpec(shape, lambda b, t: (0,) * len(shape))
    stacked = lambda shape: _layer_spec(shape, layer)
    return pl.pallas_call(
        functools.partial(_merge_mlp_kernel, final_norm=final_norm, tm=min(tm, tl)),
        grid=(bsz, seq // tl),
        in_specs=[
            tok, tok, tok, tok,
            stacked((d, d)),
            const((1, d)),
            stacked((d, D_FF)),
            stacked((D_FF, d)),
            const((1, d)),
        ],
        out_specs=tok,
        out_shape=jax.ShapeDtypeStruct((bsz, seq, d), F32),
        compiler_params=pltpu.CompilerParams(
            dimension_semantics=("arbitrary", "arbitrary"), vmem_limit_bytes=VMEM_LIMIT),
        name="merge_mlp",
    )(x, ret_o, ssd_o, gla_o, w_out, mlp_norm_w, w_up, w_down, final_norm_w)


def _regroup_kernel(w_ref, r_ref, s_ref, g_ref, *, offs):
    o_ret, o_sz, o_sdt, o_gla, o_glr, o_mg, d = offs
    rows = w_ref.shape[0]
    cols = lambda a, b: w_ref[:, a:b].astype(BF16)
    r_ref[:, 0:o_sz] = cols(o_ret, o_sz)
    r_ref[:, o_sz:o_sz + d] = cols(o_mg, o_mg + d)
    n_s = o_sdt - o_sz
    s_ref[:, 0:n_s] = cols(o_sz, o_sdt)
    dt = cols(o_sdt, o_gla)
    s_ref[:, n_s:n_s + LANE] = jnp.concatenate(
        [dt] * SSD_REP + [jnp.zeros((rows, LANE - SSD_REP * SSD_HEADS), BF16)], axis=1)
    s_ref[:, n_s + LANE:n_s + LANE + d] = cols(o_mg + d, o_mg + 2 * d)
    n_g = o_glr - o_gla
    g_ref[:, 0:n_g] = cols(o_gla, o_glr)
    g_ref[:, n_g:n_g + LANE] = jnp.concatenate(
        [cols(o_glr, o_mg), jnp.zeros((rows, LANE - GLA_RANK), BF16)], axis=1)
    g_ref[:, n_g + LANE:n_g + LANE + d] = cols(o_mg + 2 * d, o_mg + 3 * d)


def _regroup_call(w_in, offs):
    depth, d_in, width = w_in.shape
    o_ret, o_sz, o_sdt, o_gla, o_glr, o_mg, d = offs
    widths = (o_sz - o_ret + d, o_sdt - o_sz + LANE + d, o_glr - o_gla + LANE + d)
    tr = REGROUP_ROWS
    outs = pl.pallas_call(
        functools.partial(_regroup_kernel, offs=offs),
        grid=(depth * d_in // tr,),
        in_specs=[pl.BlockSpec((tr, width), lambda i: (i, 0))],
        out_specs=[pl.BlockSpec((tr, wd), lambda i: (i, 0)) for wd in widths],
        out_shape=[jax.ShapeDtypeStruct((depth * d_in, wd), BF16) for wd in widths],
        compiler_params=pltpu.CompilerParams(dimension_semantics=("arbitrary",), vmem_limit_bytes=VMEM_LIMIT),
        name="regroup",
    )(w_in.reshape(depth * d_in, width))
    return [o.reshape(depth, d_in, wd) for o, wd in zip(outs, widths)]


def kernel(x, attn_norm_w, w_in, ret_norm_w, ret_w_o, ssd_conv_w, ssd_conv_b, ssd_dt_bias, ssd_a_log, ssd_d,
           ssd_norm_w, ssd_w_o, gla_gate_w, gla_gate_b, gla_norm_w, gla_w_o, merge_gate_b, w_out, mlp_norm_w,
           w_up, w_down, final_norm_w):
    depth = w_in.shape[0]
    seq = x.shape[1]
    d = x.shape[2]
    tl = min(MIXER_TILE, seq // 2)
    tm = min(MLP_TILE, seq)
    inv_freq = ROPE_BASE ** (-jnp.arange(0, RET_DK, 2, dtype=F32) / RET_DK)
    ang = jnp.arange(seq, dtype=F32)[:, None] * inv_freq[None, :]
    cos = jnp.cos(ang)
    sin = jnp.sin(ang)
    cos2 = jnp.concatenate([cos, cos], axis=1)
    sin2 = jnp.concatenate([-sin, sin], axis=1)

    o_ret = 0
    o_sz = 4 * RET_HEADS * RET_DK
    o_sdt = o_sz + SSD_INNER + SSD_CONV_DIM
    o_gla = o_sdt + SSD_HEADS
    o_glr = o_gla + 2 * GLA_HEADS * GLA_DK + 2 * GLA_HEADS * GLA_DV
    o_mg = o_glr + GLA_RANK
    row = lambda v: v.reshape(1, -1)
    w_r, w_s, w_g = _regroup_call(w_in, (o_ret, o_sz, o_sdt, o_gla, o_glr, o_mg, d))
    ret_wo, ssd_wo, gla_wo = ret_w_o.astype(BF16), ssd_w_o.astype(BF16), gla_w_o.astype(BF16)
    w_out_b, w_up_b, w_down_b = w_out.astype(BF16), w_up.astype(BF16), w_down.astype(BF16)
    for layer in range(depth):
        mb = [row(merge_gate_b[layer, i * d:(i + 1) * d]) for i in range(3)]
        nw = row(attn_norm_w[layer])
        ret_o = _retention_call(x, nw, w_r, cos2, sin2, row(ret_norm_w[layer]), ret_wo, mb[0], tl, layer)
        ssd_o = _ssd_call(x, nw, w_s, ssd_conv_w[layer], ssd_conv_b[layer], ssd_dt_bias[layer],
                          ssd_a_log[layer], ssd_d[layer], row(ssd_norm_w[layer]), ssd_wo, mb[1], tl, layer)
        gla_o = _gla_call(x, nw, w_g, gla_gate_w[layer], gla_gate_b[layer], row(gla_norm_w[layer]),
                          gla_wo, mb[2], tl, layer)
        x = _merge_mlp_call(x, ret_o, ssd_o, gla_o, w_out_b, row(mlp_norm_w[layer]), w_up_b, w_down_b,
                            row(final_norm_w), layer == depth - 1, tm, layer)
    return x
```

```python
import functools
import math

import numpy as np
import jax
import jax.numpy as jnp
from jax import lax
from jax.experimental import pallas as pl
from jax.experimental.pallas import tpu as pltpu

F32 = jnp.float32
BF16 = jnp.bfloat16

D_MODEL = 1024
RET_HEADS = 4
RET_DK = 128
SSD_HEADS = 16
SSD_P = 64
SSD_GROUPS = 2
SSD_N = 128
SSD_INNER = SSD_HEADS * SSD_P
SSD_CONV = 4
SSD_CONV_DIM = SSD_INNER + 2 * SSD_GROUPS * SSD_N
GLA_HEADS = 4
GLA_DK = 64
GLA_DV = 128
GLA_RANK = 16
GLA_TAU = 16.0
GLA_SUB = 16
D_FF = 4 * D_MODEL
EPS = 1e-6
ROPE_BASE = 10000.0
LOG2E = math.log2(math.e)
CHUNK = 128
LANE = 128
SUBLANE = 8
SLAB = 256
SSD_REP = 3
MIXER_TILE = 256
TILES_PER_STEP = 8
MLP_TILE = 512
MLP_BLOCK = 1024
FF_CHUNK = 1024
REGROUP_ROWS = 128
VMEM_LIMIT = 56 * 1024 * 1024


def _mm(a, b):
    return lax.dot_general(a, b, (((1,), (0,)), ((), ())), preferred_element_type=F32)


def _split3(v):
    hi = v.astype(BF16)
    r1 = v - hi.astype(F32)
    mid = r1.astype(BF16)
    lo = (r1 - mid.astype(F32)).astype(BF16)
    return hi, mid, lo


def _mm_sel(sel, v):
    hi, mid, lo = _split3(v)
    return _mm(sel, hi) + _mm(sel, mid) + _mm(sel, lo)


def _rmsnorm(x, w):
    return x * lax.rsqrt(jnp.mean(x * x, axis=-1, keepdims=True) + EPS) * w


def _sigmoid(x):
    return 0.5 + 0.5 * jnp.tanh(0.5 * x)


def _silu(x):
    h = 0.5 * x
    return h + h * jnp.tanh(h)


def _softplus(x):
    return jnp.maximum(x, 0.0) + jnp.log(1.0 + jnp.exp(-jnp.abs(x)))


def _tril_bf16(n):
    r = lax.broadcasted_iota(jnp.int32, (n, n), 0)
    c = lax.broadcasted_iota(jnp.int32, (n, n), 1)
    return r >= c, (r >= c).astype(BF16)


def _const_spec(shape):
    nd = len(shape)
    return pl.BlockSpec(shape, lambda s: (0,) * nd)


def _layer_spec(shape, layer):
    nd = len(shape)
    return pl.BlockSpec((None,) + shape, lambda *_: (layer,) + (0,) * nd, pipeline_mode=pl.Buffered(1))


def _skewed_specs(bsz, seq, tl, d):
    nt = seq // tl
    tpb = math.gcd(nt, TILES_PER_STEP)
    assert seq % tl == 0 and tpb % 2 == 0
    nb = nt // tpb
    last_tile = bsz * nt - 1
    block = pl.BlockSpec((1, tpb * tl, d), lambda s: (s // nb, s % nb, 0))

    def next_tile(s):
        g = jnp.minimum((s + 1) * tpb, last_tile)
        return (g // nt, g % nt, 0)

    x_next = pl.BlockSpec((1, tl, d), next_tile)
    return tpb, nb, bsz * nb, (block, x_next), block


def _run_skewed(step, inproj, reset, x_ref, xn_ref, pa_ref, pb_ref, *, tl, tpb, nb):
    s = pl.program_id(0)

    @pl.when(s == 0)
    def _():
        inproj(x_ref[0, 0:tl, :], pa_ref)

    @pl.when(s % nb == 0)
    def _():
        reset()

    def pair(i, carry):
        k0 = 2 * i
        r1 = pl.multiple_of((k0 + 1) * tl, tl)
        step(pa_ref, 0, x_ref[0, pl.ds(r1, tl), :], pb_ref, pl.multiple_of(k0 * tl, tl))
        r2 = pl.multiple_of(jnp.minimum(k0 + 2, tpb - 1) * tl, tl)
        xn = jnp.where(i == tpb // 2 - 1, xn_ref[0], x_ref[0, pl.ds(r2, tl), :])
        step(pb_ref, 1, xn, pa_ref, pl.multiple_of((k0 + 1) * tl, tl))
        return carry

    lax.fori_loop(0, tpb // 2, pair, 0)


def _slab_emitter(hn, w_ref, pn_ref):
    wcols = w_ref.shape[1]
    slabs = [(c0, min(c0 + SLAB, wcols)) for c0 in range(0, wcols, SLAB)]

    def emit(n=None):
        for _ in range(len(slabs) if n is None else min(n, len(slabs))):
            c0, c1 = slabs.pop(0)
            pn_ref[:, c0:c1] = _mm(hn, w_ref[:, c0:c1])

    return emit, len(slabs)


def _spread(total, points):
    return [total * (i + 1) // points - total * i // points for i in range(points)]


def _gated_out(y, wo_ref, p_ref, mb_ref, col0):
    d = wo_ref.shape[1]
    return _sigmoid(p_ref[:, col0:col0 + d] + mb_ref[...]) * _mm(y, wo_ref[...])


def _mixer_params():
    return pltpu.CompilerParams(dimension_semantics=("arbitrary",), vmem_limit_bytes=VMEM_LIMIT)


def _ret_kernel(x_ref, xn_ref, nw_ref, w_ref, cos_ref, sin_ref, dec_ref, qd_ref, kd_ref, gnw_ref,
                wo_ref, mb_ref, o_ref, pa_ref, pb_ref, g_ref, st_ref, *, tl, tpb, nb, chunk_decay):
    hw = RET_HEADS * RET_DK

    def inproj(x, p_ref):
        p_ref[...] = _mm(_rmsnorm(x, nw_ref[...]).astype(BF16), w_ref[...])

    def step(p_ref, slot, xn, pn_ref, out0):
        base = slot * tl
        hn = _rmsnorm(xn, nw_ref[...]).astype(BF16)
        emit, nslab = _slab_emitter(hn, w_ref, pn_ref)
        units = [(c, hd) for c in range(tl // CHUNK) for hd in range(RET_HEADS)]
        per_unit = _spread(nslab, len(units))
        for u, (c, hd) in enumerate(units):
            rows = slice(c * CHUNK, (c + 1) * CHUNK)
            orow = slice(base + c * CHUNK, base + (c + 1) * CHUNK)
            cols = slice(hd * RET_DK, (hd + 1) * RET_DK)
            cos = cos_ref[pl.ds(out0 + c * CHUNK, CHUNK), :]
            sin = sin_ref[pl.ds(out0 + c * CHUNK, CHUNK), :]
            q = p_ref[rows, cols]
            k = p_ref[rows, hw + hd * RET_DK: hw + (hd + 1) * RET_DK]
            v = p_ref[rows, 2 * hw + hd * RET_DK: 2 * hw + (hd + 1) * RET_DK].astype(BF16)
            q = q * cos + pltpu.roll(q, RET_DK // 2, 1) * sin
            k = k * cos + pltpu.roll(k, RET_DK // 2, 1) * sin
            k_t = k.T
            scores = _mm(q.astype(BF16), k_t.astype(BF16))
            kv = _mm((k_t * kd_ref[cols, :]).astype(BF16), v)
            emit(per_unit[u])
            state = st_ref[hd]
            o = (_mm((scores * dec_ref[hd]).astype(BF16), v)
                 + _mm((q * qd_ref[:, cols]).astype(BF16), state.astype(BF16)))
            st_ref[hd] = chunk_decay[hd] * state + kv
            o = o * lax.rsqrt(jnp.mean(o * o, axis=-1, keepdims=True) + EPS) * gnw_ref[:, cols]
            gate = p_ref[rows, 3 * hw + hd * RET_DK: 3 * hw + (hd + 1) * RET_DK]
            g_ref[orow, cols] = (_silu(gate) * o).astype(BF16)
        o_ref[0, pl.ds(out0, tl), :] = _gated_out(g_ref[base:base + tl, :], wo_ref, p_ref, mb_ref,
                                                  4 * hw).astype(o_ref.dtype)

    def reset():
        st_ref[...] = jnp.zeros_like(st_ref)

    _run_skewed(step, inproj, reset, x_ref, xn_ref, pa_ref, pb_ref, tl=tl, tpb=tpb, nb=nb)


def _retention_tables():
    lg = np.log1p(-np.exp2(-5.0 - np.arange(RET_HEADS, dtype=np.float64)))
    pos = np.arange(CHUNK, dtype=np.float64)
    scale = RET_DK ** -0.5
    dist = pos[:, None] - pos[None, :]
    dec = np.where(dist >= 0, np.exp(lg[:, None, None] * np.maximum(dist, 0.0)), 0.0) * scale
    qd = np.repeat(np.exp(lg[None, :] * (pos[:, None] + 1.0)), RET_DK, axis=1)
    kd = np.repeat(np.exp(lg[:, None] * (CHUNK - 1.0 - pos[None, :])), RET_DK, axis=0) * scale
    cd = tuple(float(v) for v in np.exp(lg * CHUNK))
    return (jnp.asarray(dec, F32), jnp.asarray(qd, F32), jnp.asarray(kd, F32), cd)


def _retention_call(x, norm_w, w_r, cos2, sin2, gn_w, w_o, merge_b, tl, layer):
    bsz, seq, d = x.shape
    dec, qd, kd, cd = _retention_tables()
    hw = RET_HEADS * RET_DK
    tpb, nb, steps, x_specs, out_spec = _skewed_specs(bsz, seq, tl, d)
    rope = pl.BlockSpec((tpb * tl, RET_DK), lambda s: (s % nb, 0))
    return pl.pallas_call(
        functools.partial(_ret_kernel, tl=tl, tpb=tpb, nb=nb, chunk_decay=cd),
        grid=(steps,),
        in_specs=[
            *x_specs,
            _const_spec((1, d)),
            _layer_spec((d, 4 * hw + d), layer),
            rope, rope,
            _const_spec((RET_HEADS, CHUNK, CHUNK)),
            _const_spec((CHUNK, hw)),
            _const_spec((hw, CHUNK)),
            _const_spec((1, hw)),
            _layer_spec((hw, d), layer),
            _const_spec((1, d)),
        ],
        out_specs=out_spec,
        out_shape=jax.ShapeDtypeStruct((bsz, seq, d), BF16),
        scratch_shapes=[
            pltpu.VMEM((tl, 4 * hw + d), F32),
            pltpu.VMEM((tl, 4 * hw + d), F32),
            pltpu.VMEM((2 * tl, hw), BF16),
            pltpu.VMEM((RET_HEADS, RET_DK, RET_DK), F32),
        ],
        compiler_params=_mixer_params(),
        name="retention",
    )(x, x, norm_w, w_r, cos2, sin2, dec, qd, kd, gn_w, w_o, merge_b)


def _ssd_kernel(x_ref, xn_ref, nw_ref, w_ref, cw_ref, cb_ref, dtb_ref, alog_ref, dsk_ref, ex_ref,
                gnw_ref, wo_ref, mb_ref, o_ref, pa_ref, pb_ref, hx_ref, hb_ref, xc_ref, y_ref, st_ref, *,
                tl, tpb, nb):
    gw = SSD_INNER // SSD_GROUPS
    hpg = SSD_HEADS // SSD_GROUPS
    xbc = slice(SSD_INNER, SSD_INNER + SSD_CONV_DIM)

    def inproj(x, p_ref):
        p_ref[...] = _mm(_rmsnorm(x, nw_ref[...]).astype(BF16), w_ref[...])

    def shift_rows(v, n, hist):
        r = pltpu.roll(v, n, 0)
        sub = lax.broadcasted_iota(jnp.int32, (SUBLANE, v.shape[1]), 0)
        head = jnp.where(sub < n, pltpu.roll(hist, n, 0), r[0:SUBLANE])
        return jnp.concatenate([head, r[SUBLANE:]], axis=0)

    def step(p_ref, slot, xn, pn_ref, out0):
        base = slot * tl
        hn = _rmsnorm(xn, nw_ref[...]).astype(BF16)
        emit, _ = _slab_emitter(hn, w_ref, pn_ref)
        lane = lax.broadcasted_iota(jnp.int32, (CHUNK, LANE), 1)

        def expand(v):
            hi, mid, lo = _split3(v)
            packed = jnp.where(lane < SSD_HEADS, hi, jnp.where(lane < 2 * SSD_HEADS, mid, lo))
            return _mm(packed, ex_ref[...])

        def conv(rows):
            xin = p_ref[rows, xbc]
            x1 = shift_rows(xin, 1, hx_ref[...])
            hx_ref[...] = xin[CHUNK - SUBLANE:, :]
            near = cb_ref[...] + cw_ref[3:4, :] * xin + cw_ref[2:3, :] * x1
            far = cw_ref[1:2, :] * xin + cw_ref[0:1, :] * x1
            xc_ref[rows, :] = _silu(near + shift_rows(far, 2, hb_ref[...]))
            hb_ref[...] = far[CHUNK - SUBLANE:, :]

        dt = _softplus(p_ref[:, SSD_INNER + SSD_CONV_DIM:SSD_INNER + SSD_CONV_DIM + LANE]
                       + dtb_ref[...])
        a = dt * (-jnp.exp(alog_ref[...]))
        causal, tril = _tril_bf16(CHUNK)
        zero_b = jnp.zeros((CHUNK, LANE), BF16)
        nchunk = tl // CHUNK
        for c in range(nchunk):
            rows = slice(c * CHUNK, (c + 1) * CHUNK)
            orow = slice(base + c * CHUNK, base + (c + 1) * CHUNK)
            conv(rows)
            emit(1)
            a_cs = _mm_sel(tril, a[rows])
            dt_t = dt[rows].T
            emit(1)
            a_last = a_cs[CHUNK - 1:CHUNK, :]
            a2 = a_cs * LOG2E
            a2t = a2.T
            ecs = expand(jnp.exp(a_cs))
            dend = expand(dt[rows] * jnp.exp(a_last - a_cs))
            emit(1)
            sx = xc_ref[rows, 0:SSD_INNER]
            xs_b = sx.astype(BF16)
            xd_b = (sx * dend).astype(BF16)
            for g in range(SSD_GROUPS):
                gc = slice(g * gw, (g + 1) * gw)
                bmat_t = xc_ref[rows, SSD_INNER + g * SSD_N: SSD_INNER + (g + 1) * SSD_N].T.astype(BF16)
                cmat = xc_ref[rows, SSD_INNER + (SSD_GROUPS + g) * SSD_N:
                              SSD_INNER + (SSD_GROUPS + g + 1) * SSD_N].astype(BF16)
                cb = _mm(cmat, bmat_t)
                state = st_ref[:, gc]
                y_off = _mm(cmat, state.astype(BF16))
                upd = _mm(bmat_t, xd_b[:, gc])
                emit(1)
                cb = jnp.where(causal, cb, 0.0)
                pieces = []
                for r in range(0, hpg, 2):
                    hh = g * hpg + r
                    pair = xs_b[:, hh * SSD_P:(hh + 2) * SSD_P]
                    rhs = jnp.concatenate([jnp.where(lane < SSD_P, pair, zero_b),
                                           jnp.where(lane >= SSD_P, pair, zero_b)], axis=0)
                    lhs = []
                    for h2 in (hh, hh + 1):
                        seg = jnp.minimum(a2[:, h2:h2 + 1] - a2t[h2:h2 + 1, :], 0.0)
                        lhs.append((cb * jnp.exp2(seg) * dt_t[h2:h2 + 1, :]).astype(BF16))
                    pieces.append(_mm(jnp.concatenate(lhs, axis=1), rhs))
                y_diag = jnp.concatenate(pieces, axis=1)
                y = y_diag + y_off * ecs[:, gc] + sx[:, gc] * dsk_ref[:, gc]
                st_ref[:, gc] = ecs[CHUNK - 1:CHUNK, gc] * state + upd
                y = y * _silu(p_ref[rows, gc])
                y = y * lax.rsqrt(jnp.mean(y * y, axis=-1, keepdims=True) + EPS) * gnw_ref[:, gc]
                y_ref[orow, gc] = y.astype(BF16)
                emit(1 if c < nchunk - 1 or g < SSD_GROUPS - 1 else None)
        o_ref[0, pl.ds(out0, tl), :] = _gated_out(y_ref[base:base + tl, :], wo_ref, p_ref, mb_ref,
                                                  SSD_INNER + SSD_CONV_DIM + LANE).astype(o_ref.dtype)

    def reset():
        st_ref[...] = jnp.zeros_like(st_ref)
        hx_ref[...] = jnp.zeros_like(hx_ref)
        hb_ref[...] = jnp.zeros_like(hb_ref)

    _run_skewed(step, inproj, reset, x_ref, xn_ref, pa_ref, pb_ref, tl=tl, tpb=tpb, nb=nb)


def _ssd_call(x, norm_w, w_s, conv_w, conv_b, dt_bias, a_log, d_skip, gn_w, w_o, merge_b, tl, layer):
    bsz, seq, d = x.shape
    wcols = SSD_INNER + SSD_CONV_DIM + LANE + d
    pad = LANE - SSD_REP * SSD_HEADS
    dtb = jnp.pad(jnp.tile(dt_bias, SSD_REP), (0, pad)).reshape(1, LANE)
    alog = jnp.pad(jnp.tile(a_log, SSD_REP), (0, pad)).reshape(1, LANE)
    dsk = jnp.repeat(d_skip, SSD_P).reshape(1, SSD_INNER)
    expand = np.zeros((LANE, SSD_INNER), np.float32)
    for r in range(SSD_REP * SSD_HEADS):
        hh = r % SSD_HEADS
        expand[r, hh * SSD_P:(hh + 1) * SSD_P] = 1.0
    expand = jnp.asarray(expand, BF16)
    tpb, nb, steps, x_specs, out_spec = _skewed_specs(bsz, seq, tl, d)
    return pl.pallas_call(
        functools.partial(_ssd_kernel, tl=tl, tpb=tpb, nb=nb),
        grid=(steps,),
        in_specs=[
            *x_specs,
            _const_spec((1, d)),
            _layer_spec((d, wcols), layer),
            _const_spec((SSD_CONV, SSD_CONV_DIM)),
            _const_spec((1, SSD_CONV_DIM)),
            _const_spec((1, LANE)),
            _const_spec((1, LANE)),
            _const_spec((1, SSD_INNER)),
            _const_spec((LANE, SSD_INNER)),
            _const_spec((1, SSD_INNER)),
            _layer_spec((SSD_INNER, d), layer),
            _const_spec((1, d)),
        ],
        out_specs=out_spec,
        out_shape=jax.ShapeDtypeStruct((bsz, seq, d), BF16),
        scratch_shapes=[
            pltpu.VMEM((tl, wcols), F32),
            pltpu.VMEM((tl, wcols), F32),
            pltpu.VMEM((SUBLANE, SSD_CONV_DIM), F32),
            pltpu.VMEM((SUBLANE, SSD_CONV_DIM), F32),
            pltpu.VMEM((tl, SSD_CONV_DIM), F32),
            pltpu.VMEM((2 * tl, SSD_INNER), BF16),
            pltpu.VMEM((SSD_N, SSD_INNER), F32),
        ],
        compiler_params=_mixer_params(),
        name="ssd",
    )(x, x, norm_w, w_s, conv_w, conv_b.reshape(1, -1), dtb, alog, dsk, expand, gn_w, w_o, merge_b)


def _gla_kernel(x_ref, xn_ref, nw_ref, w_ref, gw_ref, gb_ref, gnw_ref, wo_ref, mb_ref,
                o_ref, pa_ref, pb_ref, g_ref, st_ref, *, tl, tpb, nb):
    qk = GLA_HEADS * GLA_DK
    vw = GLA_HEADS * GLA_DV
    nsub = CHUNK // GLA_SUB
    scale = GLA_DK ** -0.5

    lr0 = 2 * qk + 2 * vw

    def inproj(x, p_ref):
        p_ref[...] = _mm(_rmsnorm(x, nw_ref[...]).astype(BF16), w_ref[...])

    def step(p_ref, slot, xn, pn_ref, out0):
        base = slot * tl
        hn = _rmsnorm(xn, nw_ref[...]).astype(BF16)
        emit, _ = _slab_emitter(hn, w_ref, pn_ref)
        z = _mm(p_ref[:, lr0:lr0 + LANE].astype(BF16), gw_ref[...]) + gb_ref[...]
        emit(2)
        la = -_softplus(-z) * (LOG2E / GLA_TAU)
        causal, tril = _tril_bf16(CHUNK)
        lane_s = lax.broadcasted_iota(jnp.int32, (qk, CHUNK), 1)
        nchunk = tl // CHUNK
        bcs_all = [_mm_sel(tril, la[c * CHUNK:(c + 1) * CHUNK]) for c in range(nchunk)]
        emit(3)
        work = []
        for c in range(nchunk):
            rows = slice(c * CHUNK, (c + 1) * CHUNK)
            bcs = bcs_all[c]
            bcs_t = bcs.T
            k_t = p_ref[rows, qk:2 * qk].T
            eprev = [jnp.zeros((1, qk), F32)] + [bcs[GLA_SUB * j - 1:GLA_SUB * j, :] for j in range(1, nsub)]
            eprev_x = jnp.concatenate([jnp.broadcast_to(e, (GLA_SUB, qk)) for e in eprev], axis=0)
            q_in = p_ref[rows, 0:qk] * scale * jnp.exp2(bcs - eprev_x)
            q_in_b = q_in.astype(BF16)
            qb = (q_in * jnp.exp2(eprev_x)).astype(BF16)
            e_last_t = bcs_t[:, CHUNK - 1:CHUNK]
            kend_t = (k_t * jnp.exp2(e_last_t - bcs_t)).astype(BF16)
            work.append((rows, bcs_t, k_t, q_in_b, qb, e_last_t, kend_t))
        srows_all = []
        for rows, bcs_t, k_t, q_in_b, qb, e_last_t, kend_t in work:
            srows = [[] for _ in range(GLA_HEADS)]
            for j in range(nsub):
                start = bcs_t[:, GLA_SUB * j - 1:GLA_SUB * j] if j else 0.0
                arg = jnp.where(lane_s < GLA_SUB * (j + 1), start - bcs_t, -jnp.inf)
                khat_t = (k_t * jnp.exp2(arg)).astype(BF16)
                for hd in range(GLA_HEADS):
                    dk = slice(hd * GLA_DK, (hd + 1) * GLA_DK)
                    srows[hd].append(_mm(q_in_b[GLA_SUB * j:GLA_SUB * (j + 1), dk], khat_t[dk, :]))
                if j == nsub // 2 - 1:
                    emit(1)
            srows_all.append(srows)
        for c, (rows, bcs_t, k_t, q_in_b, qb, e_last_t, kend_t) in enumerate(work):
            orow = slice(base + c * CHUNK, base + (c + 1) * CHUNK)
            decay = jnp.exp2(e_last_t)
            for hd in range(GLA_HEADS):
                cols = slice(hd * GLA_DV, (hd + 1) * GLA_DV)
                dk = slice(hd * GLA_DK, (hd + 1) * GLA_DK)
                scores = jnp.where(causal, jnp.concatenate(srows_all[c][hd], axis=0), 0.0).astype(BF16)
                v = p_ref[rows, 2 * qk + hd * GLA_DV: 2 * qk + (hd + 1) * GLA_DV].astype(BF16)
                state = st_ref[dk, :]
                o = _mm(scores, v) + _mm(qb[:, dk], state.astype(BF16))
                st_ref[dk, :] = decay[dk, :] * state + _mm(kend_t[dk, :], v)
                o = o * lax.rsqrt(jnp.mean(o * o, axis=-1, keepdims=True) + EPS) * gnw_ref[:, cols]
                gate = p_ref[rows, 2 * qk + vw + hd * GLA_DV: 2 * qk + vw + (hd + 1) * GLA_DV]
                g_ref[orow, cols] = (_silu(gate) * o).astype(BF16)
            emit(2 if c < nchunk - 1 else None)
        o_ref[0, pl.ds(out0, tl), :] = _gated_out(g_ref[base:base + tl, :], wo_ref, p_ref, mb_ref,
                                                  lr0 + LANE).astype(o_ref.dtype)

    def reset():
        st_ref[...] = jnp.zeros_like(st_ref)

    _run_skewed(step, inproj, reset, x_ref, xn_ref, pa_ref, pb_ref, tl=tl, tpb=tpb, nb=nb)


def _gla_call(x, norm_w, w_g, gate_w, gate_b, gn_w, w_o, merge_b, tl, layer):
    bsz, seq, d = x.shape
    qk = GLA_HEADS * GLA_DK
    vw = GLA_HEADS * GLA_DV
    wcols = 2 * qk + 2 * vw + LANE + d
    gw = jnp.pad(gate_w, ((0, LANE - GLA_RANK), (0, 0))).astype(BF16)
    tpb, nb, steps, x_specs, out_spec = _skewed_specs(bsz, seq, tl, d)
    return pl.pallas_call(
        functools.partial(_gla_kernel, tl=tl, tpb=tpb, nb=nb),
        grid=(steps,),
        in_specs=[
            *x_specs,
            _const_spec((1, d)),
            _layer_spec((d, wcols), layer),
            _const_spec((LANE, qk)),
            _const_spec((1, qk)),
            _const_spec((1, vw)),
            _layer_spec((vw, d), layer),
            _const_spec((1, d)),
        ],
        out_specs=out_spec,
        out_shape=jax.ShapeDtypeStruct((bsz, seq, d), BF16),
        scratch_shapes=[
            pltpu.VMEM((tl, wcols), F32),
            pltpu.VMEM((tl, wcols), F32),
            pltpu.VMEM((2 * tl, vw), BF16),
            pltpu.VMEM((qk, GLA_DV), F32),
        ],
        compiler_params=_mixer_params(),
        name="gla",
    )(x, x, norm_w, w_g, gw, gate_b.reshape(1, -1), gn_w, w_o, merge_b)


def _merge_mlp_kernel(x_ref, a_ref, b_ref, c_ref, wout_ref, mnw_ref, wup_ref, wdn_ref, fnw_ref, o_ref, *,
                      final_norm, tm):
    def sub_tile(i, carry):
        rows = pl.ds(pl.multiple_of(i * tm, tm), tm)
        x = x_ref[0, rows, :]
        merged = (a_ref[0, rows, :].astype(F32) + b_ref[0, rows, :].astype(F32)
                  + c_ref[0, rows, :].astype(F32))
        x1 = x + _mm(merged.astype(BF16), wout_ref[...])
        h2 = _rmsnorm(x1, mnw_ref[...]).astype(BF16)
        acc = x1
        for k in range(D_FF // FF_CHUNK):
            u = jnp.maximum(_mm(h2, wup_ref[:, k * FF_CHUNK:(k + 1) * FF_CHUNK]), 0.0)
            acc = acc + _mm((u * u).astype(BF16), wdn_ref[k * FF_CHUNK:(k + 1) * FF_CHUNK, :])
        if final_norm:
            acc = _rmsnorm(acc, fnw_ref[...])
        o_ref[0, rows, :] = acc
        return carry

    lax.fori_loop(0, x_ref.shape[1] // tm, sub_tile, 0)


def _merge_mlp_call(x, ret_o, ssd_o, gla_o, w_out, mlp_norm_w, w_up, w_down, final_norm_w, final_norm, tm, layer):
    bsz, seq, d = x.shape
    tl = math.gcd(seq, MLP_BLOCK)
    tok = pl.BlockSpec((1, tl, d), lambda b, t: (b, t, 0))
    const = lambda shape: pl.BlockSpec(shape, lambda b, t: (0,) * len(shape))
    stacked = lambda shape: _layer_spec(shape, layer)
    return pl.pallas_call(
        functools.partial(_merge_mlp_kernel, final_norm=final_norm, tm=min(tm, tl)),
        grid=(bsz, seq // tl),
        in_specs=[
            tok, tok, tok, tok,
            stacked((d, d)),
            const((1, d)),
            stacked((d, D_FF)),
            stacked((D_FF, d)),
            const((1, d)),
        ],
        out_specs=tok,
        out_shape=jax.ShapeDtypeStruct((bsz, seq, d), F32),
        compiler_params=pltpu.CompilerParams(
            dimension_semantics=("arbitrary", "arbitrary"), vmem_limit_bytes=VMEM_LIMIT),
        name="merge_mlp",
    )(x, ret_o, ssd_o, gla_o, w_out, mlp_norm_w, w_up, w_down, final_norm_w)


def _regroup_kernel(w_ref, r_ref, s_ref, g_ref, *, offs):
    o_ret, o_sz, o_sdt, o_gla, o_glr, o_mg, d = offs
    rows = w_ref.shape[0]
    cols = lambda a, b: w_ref[:, a:b]
    r_ref[:, 0:o_sz] = cols(o_ret, o_sz)
    r_ref[:, o_sz:o_sz + d] = cols(o_mg, o_mg + d)
    n_s = o_sdt - o_sz
    s_ref[:, 0:n_s] = cols(o_sz, o_sdt)
    dt = cols(o_sdt, o_gla)
    s_ref[:, n_s:n_s + LANE] = jnp.concatenate(
        [dt] * SSD_REP + [jnp.zeros((rows, LANE - SSD_REP * SSD_HEADS), BF16)], axis=1)
    s_ref[:, n_s + LANE:n_s + LANE + d] = cols(o_mg + d, o_mg + 2 * d)
    n_g = o_glr - o_gla
    g_ref[:, 0:n_g] = cols(o_gla, o_glr)
    g_ref[:, n_g:n_g + LANE] = jnp.concatenate(
        [cols(o_glr, o_mg), jnp.zeros((rows, LANE - GLA_RANK), BF16)], axis=1)
    g_ref[:, n_g + LANE:n_g + LANE + d] = cols(o_mg + 2 * d, o_mg + 3 * d)


def _regroup_call(w_in, offs):
    depth, d_in, width = w_in.shape
    o_ret, o_sz, o_sdt, o_gla, o_glr, o_mg, d = offs
    widths = (o_sz - o_ret + d, o_sdt - o_sz + LANE + d, o_glr - o_gla + LANE + d)
    tr = REGROUP_ROWS
    outs = pl.pallas_call(
        functools.partial(_regroup_kernel, offs=offs),
        grid=(depth * d_in // tr,),
        in_specs=[pl.BlockSpec((tr, width), lambda i: (i, 0))],
        out_specs=[pl.BlockSpec((tr, wd), lambda i: (i, 0)) for wd in widths],
        out_shape=[jax.ShapeDtypeStruct((depth * d_in, wd), BF16) for wd in widths],
        compiler_params=pltpu.CompilerParams(dimension_semantics=("arbitrary",), vmem_limit_bytes=VMEM_LIMIT),
        name="regroup",
    )(w_in.astype(BF16).reshape(depth * d_in, width))
    return [o.reshape(depth, d_in, wd) for o, wd in zip(outs, widths)]


def kernel(x, attn_norm_w, w_in, ret_norm_w, ret_w_o, ssd_conv_w, ssd_conv_b, ssd_dt_bias, ssd_a_log, ssd_d,
           ssd_norm_w, ssd_w_o, gla_gate_w, gla_gate_b, gla_norm_w, gla_w_o, merge_gate_b, w_out, mlp_norm_w,
           w_up, w_down, final_norm_w):
    depth = w_in.shape[0]
    seq = x.shape[1]
    d = x.shape[2]
    tl = min(MIXER_TILE, seq // 2)
    tm = min(MLP_TILE, seq)
    inv_freq = ROPE_BASE ** (-jnp.arange(0, RET_DK, 2, dtype=F32) / RET_DK)
    ang = jnp.arange(seq, dtype=F32)[:, None] * inv_freq[None, :]
    cos = jnp.cos(ang)
    sin = jnp.sin(ang)
    cos2 = jnp.concatenate([cos, cos], axis=1)
    sin2 = jnp.concatenate([-sin, sin], axis=1)

    o_ret = 0
    o_sz = 4 * RET_HEADS * RET_DK
    o_sdt = o_sz + SSD_INNER + SSD_CONV_DIM
    o_gla = o_sdt + SSD_HEADS
    o_glr = o_gla + 2 * GLA_HEADS * GLA_DK + 2 * GLA_HEADS * GLA_DV
    o_mg = o_glr + GLA_RANK
    row = lambda v: v.reshape(1, -1)
    w_r, w_s, w_g = _regroup_call(w_in, (o_ret, o_sz, o_sdt, o_gla, o_glr, o_mg, d))
    ret_wo, ssd_wo, gla_wo = ret_w_o.astype(BF16), ssd_w_o.astype(BF16), gla_w_o.astype(BF16)
    w_out_b, w_up_b, w_down_b = w_out.astype(BF16), w_up.astype(BF16), w_down.astype(BF16)
    for layer in range(depth):
        mb = [row(merge_gate_b[layer, i * d:(i + 1) * d]) for i in range(3)]
        nw = row(attn_norm_w[layer])
        ret_o = _retention_call(x, nw, w_r, cos2, sin2, row(ret_norm_w[layer]), ret_wo, mb[0], tl, layer)
        ssd_o = _ssd_call(x, nw, w_s, ssd_conv_w[layer], ssd_conv_b[layer], ssd_dt_bias[layer],
                          ssd_a_log[layer], ssd_d[layer], row(ssd_norm_w[layer]), ssd_wo, mb[1], tl, layer)
        gla_o = _gla_call(x, nw, w_g, gla_gate_w[layer], gla_gate_b[layer], row(gla_norm_w[layer]),
                          gla_wo, mb[2], tl, layer)
        x = _merge_mlp_call(x, ret_o, ssd_o, gla_o, w_out_b, row(mlp_norm_w[layer]), w_up_b, w_down_b,
                            row(final_norm_w), layer == depth - 1, tm, layer)
    return x
```

```python
import functools
import math

import numpy as np
import jax
import jax.numpy as jnp
from jax import lax
from jax.experimental import pallas as pl
from jax.experimental.pallas import tpu as pltpu

F32 = jnp.float32
BF16 = jnp.bfloat16

D_MODEL = 1024
RET_HEADS = 4
RET_DK = 128
SSD_HEADS = 16
SSD_P = 64
SSD_GROUPS = 2
SSD_N = 128
SSD_INNER = SSD_HEADS * SSD_P
SSD_CONV = 4
SSD_CONV_DIM = SSD_INNER + 2 * SSD_GROUPS * SSD_N
GLA_HEADS = 4
GLA_DK = 64
GLA_DV = 128
GLA_RANK = 16
GLA_TAU = 16.0
GLA_SUB = 16
D_FF = 4 * D_MODEL
EPS = 1e-6
ROPE_BASE = 10000.0
LOG2E = math.log2(math.e)
CHUNK = 128
LANE = 128
SUBLANE = 8
SLAB = 256
SSD_REP = 3
MIXER_TILE = 256
TILES_PER_STEP = 8
MLP_TILE = 512
MLP_BLOCK = 1024
FF_CHUNK = 1024
REGROUP_ROWS = 128
VMEM_LIMIT = 56 * 1024 * 1024


def _mm(a, b):
    return lax.dot_general(a, b, (((1,), (0,)), ((), ())), preferred_element_type=F32)


def _split3(v):
    hi = v.astype(BF16)
    r1 = v - hi.astype(F32)
    mid = r1.astype(BF16)
    lo = (r1 - mid.astype(F32)).astype(BF16)
    return hi, mid, lo


def _mm_sel(sel, v):
    hi = v.astype(BF16)
    mid = (v - hi.astype(F32)).astype(BF16)
    return _mm(sel, hi) + _mm(sel, mid)


def _rmsnorm(x, w):
    return x * lax.rsqrt(jnp.mean(x * x, axis=-1, keepdims=True) + EPS) * w


def _sigmoid(x):
    return 0.5 + 0.5 * jnp.tanh(0.5 * x)


def _silu(x):
    h = 0.5 * x
    return h + h * jnp.tanh(h)


def _softplus(x):
    return jnp.maximum(x, 0.0) + jnp.log(1.0 + jnp.exp(-jnp.abs(x)))


def _tril_bf16(n):
    r = lax.broadcasted_iota(jnp.int32, (n, n), 0)
    c = lax.broadcasted_iota(jnp.int32, (n, n), 1)
    return r >= c, (r >= c).astype(BF16)


def _const_spec(shape):
    nd = len(shape)
    return pl.BlockSpec(shape, lambda s: (0,) * nd)


def _layer_spec(shape, layer):
    nd = len(shape)
    return pl.BlockSpec((None,) + shape, lambda *_: (layer,) + (0,) * nd, pipeline_mode=pl.Buffered(1))


def _skewed_specs(bsz, seq, tl, d):
    nt = seq // tl
    tpb = math.gcd(nt, TILES_PER_STEP)
    assert seq % tl == 0 and tpb % 2 == 0
    nb = nt // tpb
    last_tile = bsz * nt - 1
    block = pl.BlockSpec((1, tpb * tl, d), lambda s: (s // nb, s % nb, 0))

    def next_tile(s):
        g = jnp.minimum((s + 1) * tpb, last_tile)
        return (g // nt, g % nt, 0)

    x_next = pl.BlockSpec((1, tl, d), next_tile)
    return tpb, nb, bsz * nb, (block, x_next), block


def _run_skewed(step, inproj, reset, x_ref, xn_ref, pa_ref, pb_ref, *, tl, tpb, nb):
    s = pl.program_id(0)

    @pl.when(s == 0)
    def _():
        inproj(x_ref[0, 0:tl, :], pa_ref)

    @pl.when(s % nb == 0)
    def _():
        reset()

    def pair(i, carry):
        k0 = 2 * i
        r1 = pl.multiple_of((k0 + 1) * tl, tl)
        step(pa_ref, 0, x_ref[0, pl.ds(r1, tl), :], pb_ref, pl.multiple_of(k0 * tl, tl))
        r2 = pl.multiple_of(jnp.minimum(k0 + 2, tpb - 1) * tl, tl)
        xn = jnp.where(i == tpb // 2 - 1, xn_ref[0], x_ref[0, pl.ds(r2, tl), :])
        step(pb_ref, 1, xn, pa_ref, pl.multiple_of((k0 + 1) * tl, tl))
        return carry

    lax.fori_loop(0, tpb // 2, pair, 0)


def _slab_emitter(hn, w_ref, pn_ref):
    wcols = w_ref.shape[1]
    slabs = [(c0, min(c0 + SLAB, wcols)) for c0 in range(0, wcols, SLAB)]

    def emit(n=None):
        for _ in range(len(slabs) if n is None else min(n, len(slabs))):
            c0, c1 = slabs.pop(0)
            pn_ref[:, c0:c1] = _mm(hn, w_ref[:, c0:c1])

    return emit, len(slabs)


def _spread(total, points):
    return [total * (i + 1) // points - total * i // points for i in range(points)]


def _gated_out(y, wo_ref, p_ref, mb_ref, col0):
    d = wo_ref.shape[1]
    return _sigmoid(p_ref[:, col0:col0 + d] + mb_ref[...]) * _mm(y, wo_ref[...])


def _mixer_params():
    return pltpu.CompilerParams(dimension_semantics=("arbitrary",), vmem_limit_bytes=VMEM_LIMIT)


def _ret_kernel(x_ref, xn_ref, nw_ref, w_ref, cos_ref, sin_ref, dec_ref, qd_ref, kd_ref, gnw_ref,
                wo_ref, mb_ref, o_ref, pa_ref, pb_ref, g_ref, st_ref, *, tl, tpb, nb, chunk_decay):
    hw = RET_HEADS * RET_DK

    def inproj(x, p_ref):
        p_ref[...] = _mm(_rmsnorm(x, nw_ref[...]).astype(BF16), w_ref[...])

    def step(p_ref, slot, xn, pn_ref, out0):
        base = slot * tl
        hn = _rmsnorm(xn, nw_ref[...]).astype(BF16)
        emit, nslab = _slab_emitter(hn, w_ref, pn_ref)
        units = [(c, hd) for c in range(tl // CHUNK) for hd in range(RET_HEADS)]
        per_unit = _spread(nslab, len(units))
        for u, (c, hd) in enumerate(units):
            rows = slice(c * CHUNK, (c + 1) * CHUNK)
            orow = slice(base + c * CHUNK, base + (c + 1) * CHUNK)
            cols = slice(hd * RET_DK, (hd + 1) * RET_DK)
            cos = cos_ref[pl.ds(out0 + c * CHUNK, CHUNK), :]
            sin = sin_ref[pl.ds(out0 + c * CHUNK, CHUNK), :]
            q = p_ref[rows, cols]
            k = p_ref[rows, hw + hd * RET_DK: hw + (hd + 1) * RET_DK]
            v = p_ref[rows, 2 * hw + hd * RET_DK: 2 * hw + (hd + 1) * RET_DK].astype(BF16)
            q = q * cos + pltpu.roll(q, RET_DK // 2, 1) * sin
            k = k * cos + pltpu.roll(k, RET_DK // 2, 1) * sin
            k_t = k.T
            scores = _mm(q.astype(BF16), k_t.astype(BF16))
            kv = _mm((k_t * kd_ref[cols, :]).astype(BF16), v)
            emit(per_unit[u])
            state = st_ref[hd]
            o = (_mm((scores * dec_ref[hd]).astype(BF16), v)
                 + _mm((q * qd_ref[:, cols]).astype(BF16), state.astype(BF16)))
            st_ref[hd] = chunk_decay[hd] * state + kv
            o = o * lax.rsqrt(jnp.mean(o * o, axis=-1, keepdims=True) + EPS) * gnw_ref[:, cols]
            gate = p_ref[rows, 3 * hw + hd * RET_DK: 3 * hw + (hd + 1) * RET_DK]
            g_ref[orow, cols] = (_silu(gate) * o).astype(BF16)
        o_ref[0, pl.ds(out0, tl), :] = _gated_out(g_ref[base:base + tl, :], wo_ref, p_ref, mb_ref,
                                                  4 * hw).astype(o_ref.dtype)

    def reset():
        st_ref[...] = jnp.zeros_like(st_ref)

    _run_skewed(step, inproj, reset, x_ref, xn_ref, pa_ref, pb_ref, tl=tl, tpb=tpb, nb=nb)


def _retention_tables():
    lg = np.log1p(-np.exp2(-5.0 - np.arange(RET_HEADS, dtype=np.float64)))
    pos = np.arange(CHUNK, dtype=np.float64)
    scale = RET_DK ** -0.5
    dist = pos[:, None] - pos[None, :]
    dec = np.where(dist >= 0, np.exp(lg[:, None, None] * np.maximum(dist, 0.0)), 0.0) * scale
    qd = np.repeat(np.exp(lg[None, :] * (pos[:, None] + 1.0)), RET_DK, axis=1)
    kd = np.repeat(np.exp(lg[:, None] * (CHUNK - 1.0 - pos[None, :])), RET_DK, axis=0) * scale
    cd = tuple(float(v) for v in np.exp(lg * CHUNK))
    return (jnp.asarray(dec, F32), jnp.asarray(qd, F32), jnp.asarray(kd, F32), cd)


def _retention_call(x, norm_w, w_r, cos2, sin2, gn_w, w_o, merge_b, tl, layer):
    bsz, seq, d = x.shape
    dec, qd, kd, cd = _retention_tables()
    hw = RET_HEADS * RET_DK
    tpb, nb, steps, x_specs, out_spec = _skewed_specs(bsz, seq, tl, d)
    rope = pl.BlockSpec((tpb * tl, RET_DK), lambda s: (s % nb, 0))
    return pl.pallas_call(
        functools.partial(_ret_kernel, tl=tl, tpb=tpb, nb=nb, chunk_decay=cd),
        grid=(steps,),
        in_specs=[
            *x_specs,
            _const_spec((1, d)),
            _layer_spec((d, 4 * hw + d), layer),
            rope, rope,
            _const_spec((RET_HEADS, CHUNK, CHUNK)),
            _const_spec((CHUNK, hw)),
            _const_spec((hw, CHUNK)),
            _const_spec((1, hw)),
            _layer_spec((hw, d), layer),
            _const_spec((1, d)),
        ],
        out_specs=out_spec,
        out_shape=jax.ShapeDtypeStruct((bsz, seq, d), BF16),
        scratch_shapes=[
            pltpu.VMEM((tl, 4 * hw + d), F32),
            pltpu.VMEM((tl, 4 * hw + d), F32),
            pltpu.VMEM((2 * tl, hw), BF16),
            pltpu.VMEM((RET_HEADS, RET_DK, RET_DK), F32),
        ],
        compiler_params=_mixer_params(),
        name="retention",
    )(x, x, norm_w, w_r, cos2, sin2, dec, qd, kd, gn_w, w_o, merge_b)


def _ssd_kernel(x_ref, xn_ref, nw_ref, w_ref, cw_ref, cb_ref, dtb_ref, alog_ref, dsk_ref, ex_ref,
                gnw_ref, wo_ref, mb_ref, o_ref, pa_ref, pb_ref, hx_ref, hb_ref, xc_ref, y_ref, st_ref, *,
                tl, tpb, nb):
    gw = SSD_INNER // SSD_GROUPS
    hpg = SSD_HEADS // SSD_GROUPS
    xbc = slice(SSD_INNER, SSD_INNER + SSD_CONV_DIM)

    def inproj(x, p_ref):
        p_ref[...] = _mm(_rmsnorm(x, nw_ref[...]).astype(BF16), w_ref[...])

    def shift_rows(v, n, hist):
        r = pltpu.roll(v, n, 0)
        sub = lax.broadcasted_iota(jnp.int32, (SUBLANE, v.shape[1]), 0)
        head = jnp.where(sub < n, pltpu.roll(hist, n, 0), r[0:SUBLANE])
        return jnp.concatenate([head, r[SUBLANE:]], axis=0)

    def step(p_ref, slot, xn, pn_ref, out0):
        base = slot * tl
        hn = _rmsnorm(xn, nw_ref[...]).astype(BF16)
        emit, _ = _slab_emitter(hn, w_ref, pn_ref)
        lane = lax.broadcasted_iota(jnp.int32, (CHUNK, LANE), 1)

        def expand(v):
            hi, mid, lo = _split3(v)
            packed = jnp.where(lane < SSD_HEADS, hi, jnp.where(lane < 2 * SSD_HEADS, mid, lo))
            return _mm(packed, ex_ref[...])

        def conv(rows):
            xin = p_ref[rows, xbc]
            x1 = shift_rows(xin, 1, hx_ref[...])
            hx_ref[...] = xin[CHUNK - SUBLANE:, :]
            near = cb_ref[...] + cw_ref[3:4, :] * xin + cw_ref[2:3, :] * x1
            far = cw_ref[1:2, :] * xin + cw_ref[0:1, :] * x1
            xc_ref[rows, :] = _silu(near + shift_rows(far, 2, hb_ref[...]))
            hb_ref[...] = far[CHUNK - SUBLANE:, :]

        dt = _softplus(p_ref[:, SSD_INNER + SSD_CONV_DIM:SSD_INNER + SSD_CONV_DIM + LANE]
                       + dtb_ref[...])
        a = dt * (-jnp.exp(alog_ref[...]))
        causal, tril = _tril_bf16(CHUNK)
        zero_b = jnp.zeros((CHUNK, LANE), BF16)
        nchunk = tl // CHUNK
        for c in range(nchunk):
            rows = slice(c * CHUNK, (c + 1) * CHUNK)
            orow = slice(base + c * CHUNK, base + (c + 1) * CHUNK)
            conv(rows)
            emit(1)
            a_cs = _mm_sel(tril, a[rows])
            dt_t = dt[rows].T
            emit(1)
            a_last = a_cs[CHUNK - 1:CHUNK, :]
            a2 = a_cs * LOG2E
            a2t = a2.T
            ecs = expand(jnp.exp(a_cs))
            dend = expand(dt[rows] * jnp.exp(a_last - a_cs))
            emit(1)
            sx = xc_ref[rows, 0:SSD_INNER]
            xs_b = sx.astype(BF16)
            xd_b = (sx * dend).astype(BF16)
            for g in range(SSD_GROUPS):
                gc = slice(g * gw, (g + 1) * gw)
                bmat_t = xc_ref[rows, SSD_INNER + g * SSD_N: SSD_INNER + (g + 1) * SSD_N].T.astype(BF16)
                cmat = xc_ref[rows, SSD_INNER + (SSD_GROUPS + g) * SSD_N:
                              SSD_INNER + (SSD_GROUPS + g + 1) * SSD_N].astype(BF16)
                cb = _mm(cmat, bmat_t)
                state = st_ref[:, gc]
                y_off = _mm(cmat, state.astype(BF16))
                upd = _mm(bmat_t, xd_b[:, gc])
                emit(1)
                cb = jnp.where(causal, cb, 0.0)
                pieces = []
                for r in range(0, hpg, 2):
                    hh = g * hpg + r
                    pair = xs_b[:, hh * SSD_P:(hh + 2) * SSD_P]
                    rhs = jnp.concatenate([jnp.where(lane < SSD_P, pair, zero_b),
                                           jnp.where(lane >= SSD_P, pair, zero_b)], axis=0)
                    lhs = []
                    for h2 in (hh, hh + 1):
                        seg = jnp.minimum(a2[:, h2:h2 + 1] - a2t[h2:h2 + 1, :], 0.0)
                        lhs.append((cb * jnp.exp2(seg) * dt_t[h2:h2 + 1, :]).astype(BF16))
                    pieces.append(_mm(jnp.concatenate(lhs, axis=1), rhs))
                y_diag = jnp.concatenate(pieces, axis=1)
                y = y_diag + y_off * ecs[:, gc] + sx[:, gc] * dsk_ref[:, gc]
                st_ref[:, gc] = ecs[CHUNK - 1:CHUNK, gc] * state + upd
                y = y * _silu(p_ref[rows, gc])
                y = y * lax.rsqrt(jnp.mean(y * y, axis=-1, keepdims=True) + EPS) * gnw_ref[:, gc]
                y_ref[orow, gc] = y.astype(BF16)
                emit(1 if c < nchunk - 1 or g < SSD_GROUPS - 1 else None)
        o_ref[0, pl.ds(out0, tl), :] = _gated_out(y_ref[base:base + tl, :], wo_ref, p_ref, mb_ref,
                                                  SSD_INNER + SSD_CONV_DIM + LANE).astype(o_ref.dtype)

    def reset():
        st_ref[...] = jnp.zeros_like(st_ref)
        hx_ref[...] = jnp.zeros_like(hx_ref)
        hb_ref[...] = jnp.zeros_like(hb_ref)

    _run_skewed(step, inproj, reset, x_ref, xn_ref, pa_ref, pb_ref, tl=tl, tpb=tpb, nb=nb)


def _ssd_call(x, norm_w, w_s, conv_w, conv_b, dt_bias, a_log, d_skip, gn_w, w_o, merge_b, tl, layer):
    bsz, seq, d = x.shape
    wcols = SSD_INNER + SSD_CONV_DIM + LANE + d
    pad = LANE - SSD_REP * SSD_HEADS
    dtb = jnp.pad(jnp.tile(dt_bias, SSD_REP), (0, pad)).reshape(1, LANE)
    alog = jnp.pad(jnp.tile(a_log, SSD_REP), (0, pad)).reshape(1, LANE)
    dsk = jnp.repeat(d_skip, SSD_P).reshape(1, SSD_INNER)
    expand = np.zeros((LANE, SSD_INNER), np.float32)
    for r in range(SSD_REP * SSD_HEADS):
        hh = r % SSD_HEADS
        expand[r, hh * SSD_P:(hh + 1) * SSD_P] = 1.0
    expand = jnp.asarray(expand, BF16)
    tpb, nb, steps, x_specs, out_spec = _skewed_specs(bsz, seq, tl, d)
    return pl.pallas_call(
        functools.partial(_ssd_kernel, tl=tl, tpb=tpb, nb=nb),
        grid=(steps,),
        in_specs=[
            *x_specs,
            _const_spec((1, d)),
            _layer_spec((d, wcols), layer),
            _const_spec((SSD_CONV, SSD_CONV_DIM)),
            _const_spec((1, SSD_CONV_DIM)),
            _const_spec((1, LANE)),
            _const_spec((1, LANE)),
            _const_spec((1, SSD_INNER)),
            _const_spec((LANE, SSD_INNER)),
            _const_spec((1, SSD_INNER)),
            _layer_spec((SSD_INNER, d), layer),
            _const_spec((1, d)),
        ],
        out_specs=out_spec,
        out_shape=jax.ShapeDtypeStruct((bsz, seq, d), BF16),
        scratch_shapes=[
            pltpu.VMEM((tl, wcols), F32),
            pltpu.VMEM((tl, wcols), F32),
            pltpu.VMEM((SUBLANE, SSD_CONV_DIM), F32),
            pltpu.VMEM((SUBLANE, SSD_CONV_DIM), F32),
            pltpu.VMEM((tl, SSD_CONV_DIM), F32),
            pltpu.VMEM((2 * tl, SSD_INNER), BF16),
            pltpu.VMEM((SSD_N, SSD_INNER), F32),
        ],
        compiler_params=_mixer_params(),
        name="ssd",
    )(x, x, norm_w, w_s, conv_w, conv_b.reshape(1, -1), dtb, alog, dsk, expand, gn_w, w_o, merge_b)


def _gla_kernel(x_ref, xn_ref, nw_ref, w_ref, gw_ref, gb_ref, gnw_ref, wo_ref, mb_ref,
                o_ref, pa_ref, pb_ref, g_ref, st_ref, *, tl, tpb, nb):
    qk = GLA_HEADS * GLA_DK
    vw = GLA_HEADS * GLA_DV
    nsub = CHUNK // GLA_SUB
    scale = GLA_DK ** -0.5

    lr0 = 2 * qk + 2 * vw

    def inproj(x, p_ref):
        p_ref[...] = _mm(_rmsnorm(x, nw_ref[...]).astype(BF16), w_ref[...])

    def step(p_ref, slot, xn, pn_ref, out0):
        base = slot * tl
        hn = _rmsnorm(xn, nw_ref[...]).astype(BF16)
        emit, _ = _slab_emitter(hn, w_ref, pn_ref)
        z = _mm(p_ref[:, lr0:lr0 + LANE].astype(BF16), gw_ref[...]) + gb_ref[...]
        emit(2)
        la = -_softplus(-z) * (LOG2E / GLA_TAU)
        causal, tril = _tril_bf16(CHUNK)
        lane_s = lax.broadcasted_iota(jnp.int32, (qk, CHUNK), 1)
        nchunk = tl // CHUNK
        bcs_all = [_mm_sel(tril, la[c * CHUNK:(c + 1) * CHUNK]) for c in range(nchunk)]
        emit(3)
        work = []
        for c in range(nchunk):
            rows = slice(c * CHUNK, (c + 1) * CHUNK)
            bcs = bcs_all[c]
            bcs_t = bcs.T
            k_t = p_ref[rows, qk:2 * qk].T
            eprev = [jnp.zeros((1, qk), F32)] + [bcs[GLA_SUB * j - 1:GLA_SUB * j, :] for j in range(1, nsub)]
            eprev_x = jnp.concatenate([jnp.broadcast_to(e, (GLA_SUB, qk)) for e in eprev], axis=0)
            q_in = p_ref[rows, 0:qk] * scale * jnp.exp2(bcs - eprev_x)
            q_in_b = q_in.astype(BF16)
            qb = (q_in * jnp.exp2(eprev_x)).astype(BF16)
            e_last_t = bcs_t[:, CHUNK - 1:CHUNK]
            kend_t = (k_t * jnp.exp2(e_last_t - bcs_t)).astype(BF16)
            work.append((rows, bcs_t, k_t, q_in_b, qb, e_last_t, kend_t))
        srows_all = []
        for rows, bcs_t, k_t, q_in_b, qb, e_last_t, kend_t in work:
            srows = [[] for _ in range(GLA_HEADS)]
            for j in range(nsub):
                start = bcs_t[:, GLA_SUB * j - 1:GLA_SUB * j] if j else 0.0
                arg = jnp.where(lane_s < GLA_SUB * (j + 1), start - bcs_t, -jnp.inf)
                khat_t = (k_t * jnp.exp2(arg)).astype(BF16)
                for hd in range(GLA_HEADS):
                    dk = slice(hd * GLA_DK, (hd + 1) * GLA_DK)
                    srows[hd].append(_mm(q_in_b[GLA_SUB * j:GLA_SUB * (j + 1), dk], khat_t[dk, :]))
                if j == nsub // 2 - 1:
                    emit(1)
            srows_all.append(srows)
        for c, (rows, bcs_t, k_t, q_in_b, qb, e_last_t, kend_t) in enumerate(work):
            orow = slice(base + c * CHUNK, base + (c + 1) * CHUNK)
            decay = jnp.exp2(e_last_t)
            for hd in range(GLA_HEADS):
                cols = slice(hd * GLA_DV, (hd + 1) * GLA_DV)
                dk = slice(hd * GLA_DK, (hd + 1) * GLA_DK)
                scores = jnp.where(causal, jnp.concatenate(srows_all[c][hd], axis=0), 0.0).astype(BF16)
                v = p_ref[rows, 2 * qk + hd * GLA_DV: 2 * qk + (hd + 1) * GLA_DV].astype(BF16)
                state = st_ref[dk, :]
                o = _mm(scores, v) + _mm(qb[:, dk], state.astype(BF16))
                st_ref[dk, :] = decay[dk, :] * state + _mm(kend_t[dk, :], v)
                o = o * lax.rsqrt(jnp.mean(o * o, axis=-1, keepdims=True) + EPS) * gnw_ref[:, cols]
                gate = p_ref[rows, 2 * qk + vw + hd * GLA_DV: 2 * qk + vw + (hd + 1) * GLA_DV]
                g_ref[orow, cols] = (_silu(gate) * o).astype(BF16)
            emit(2 if c < nchunk - 1 else None)
        o_ref[0, pl.ds(out0, tl), :] = _gated_out(g_ref[base:base + tl, :], wo_ref, p_ref, mb_ref,
                                                  lr0 + LANE).astype(o_ref.dtype)

    def reset():
        st_ref[...] = jnp.zeros_like(st_ref)

    _run_skewed(step, inproj, reset, x_ref, xn_ref, pa_ref, pb_ref, tl=tl, tpb=tpb, nb=nb)


def _gla_call(x, norm_w, w_g, gate_w, gate_b, gn_w, w_o, merge_b, tl, layer):
    bsz, seq, d = x.shape
    qk = GLA_HEADS * GLA_DK
    vw = GLA_HEADS * GLA_DV
    wcols = 2 * qk + 2 * vw + LANE + d
    gw = jnp.pad(gate_w, ((0, LANE - GLA_RANK), (0, 0))).astype(BF16)
    tpb, nb, steps, x_specs, out_spec = _skewed_specs(bsz, seq, tl, d)
    return pl.pallas_call(
        functools.partial(_gla_kernel, tl=tl, tpb=tpb, nb=nb),
        grid=(steps,),
        in_specs=[
            *x_specs,
            _const_spec((1, d)),
            _layer_spec((d, wcols), layer),
            _const_spec((LANE, qk)),
            _const_spec((1, qk)),
            _const_spec((1, vw)),
            _layer_spec((vw, d), layer),
            _const_spec((1, d)),
        ],
        out_specs=out_spec,
        out_shape=jax.ShapeDtypeStruct((bsz, seq, d), BF16),
        scratch_shapes=[
            pltpu.VMEM((tl, wcols), F32),
            pltpu.VMEM((tl, wcols), F32),
            pltpu.VMEM((2 * tl, vw), BF16),
            pltpu.VMEM((qk, GLA_DV), F32),
        ],
        compiler_params=_mixer_params(),
        name="gla",
    )(x, x, norm_w, w_g, gw, gate_b.reshape(1, -1), gn_w, w_o, merge_b)


def _merge_mlp_kernel(x_ref, a_ref, b_ref, c_ref, wout_ref, mnw_ref, wup_ref, wdn_ref, fnw_ref, o_ref, *,
                      final_norm, tm):
    def sub_tile(i, carry):
        rows = pl.ds(pl.multiple_of(i * tm, tm), tm)
        x = x_ref[0, rows, :]
        merged = (a_ref[0, rows, :].astype(F32) + b_ref[0, rows, :].astype(F32)
                  + c_ref[0, rows, :].astype(F32))
        x1 = x + _mm(merged.astype(BF16), wout_ref[...])
        h2 = _rmsnorm(x1, mnw_ref[...]).astype(BF16)
        acc = x1
        for k in range(D_FF // FF_CHUNK):
            u = jnp.maximum(_mm(h2, wup_ref[:, k * FF_CHUNK:(k + 1) * FF_CHUNK]), 0.0)
            acc = acc + _mm((u * u).astype(BF16), wdn_ref[k * FF_CHUNK:(k + 1) * FF_CHUNK, :])
        if final_norm:
            acc = _rmsnorm(acc, fnw_ref[...])
        o_ref[0, rows, :] = acc
        return carry

    lax.fori_loop(0, x_ref.shape[1] // tm, sub_tile, 0)


def _merge_mlp_call(x, ret_o, ssd_o, gla_o, w_out, mlp_norm_w, w_up, w_down, final_norm_w, final_norm, tm, layer):
    bsz, seq, d = x.shape
    tl = math.gcd(seq, MLP_BLOCK)
    tok = pl.BlockSpec((1, tl, d), lambda b, t: (b, t, 0))
    const = lambda shape: pl.BlockSpec(shape, lambda b, t: (0,) * len(shape))
    stacked = lambda shape: _layer_spec(shape, layer)
    return pl.pallas_call(
        functools.partial(_merge_mlp_kernel, final_norm=final_norm, tm=min(tm, tl)),
        grid=(bsz, seq // tl),
        in_specs=[
            tok, tok, tok, tok,
            stacked((d, d)),
            const((1, d)),
            stacked((d, D_FF)),
            stacked((D_FF, d)),
            const((1, d)),
        ],
        out_specs=tok,
        out_shape=jax.ShapeDtypeStruct((bsz, seq, d), F32),
        compiler_params=pltpu.CompilerParams(
            dimension_semantics=("arbitrary", "arbitrary"), vmem_limit_bytes=VMEM_LIMIT),
        name="merge_mlp",
    )(x, ret_o, ssd_o, gla_o, w_out, mlp_norm_w, w_up, w_down, final_norm_w)


def _regroup_kernel(w_ref, r_ref, s_ref, g_ref, *, offs):
    o_ret, o_sz, o_sdt, o_gla, o_glr, o_mg, d = offs
    rows = w_ref.shape[0]
    cols = lambda a, b: w_ref[:, a:b]
    r_ref[:, 0:o_sz] = cols(o_ret, o_sz)
    r_ref[:, o_sz:o_sz + d] = cols(o_mg, o_mg + d)
    n_s = o_sdt - o_sz
    s_ref[:, 0:n_s] = cols(o_sz, o_sdt)
    dt = cols(o_sdt, o_gla)
    s_ref[:, n_s:n_s + LANE] = jnp.concatenate(
        [dt] * SSD_REP + [jnp.zeros((rows, LANE - SSD_REP * SSD_HEADS), BF16)], axis=1)
    s_ref[:, n_s + LANE:n_s + LANE + d] = cols(o_mg + d, o_mg + 2 * d)
    n_g = o_glr - o_gla
    g_ref[:, 0:n_g] = cols(o_gla, o_glr)
    g_ref[:, n_g:n_g + LANE] = jnp.concatenate(
        [cols(o_glr, o_mg), jnp.zeros((rows, LANE - GLA_RANK), BF16)], axis=1)
    g_ref[:, n_g + LANE:n_g + LANE + d] = cols(o_mg + 2 * d, o_mg + 3 * d)


def _regroup_call(w_in, offs):
    depth, d_in, width = w_in.shape
    o_ret, o_sz, o_sdt, o_gla, o_glr, o_mg, d = offs
    widths = (o_sz - o_ret + d, o_sdt - o_sz + LANE + d, o_glr - o_gla + LANE + d)
    tr = REGROUP_ROWS
    outs = pl.pallas_call(
        functools.partial(_regroup_kernel, offs=offs),
        grid=(depth * d_in // tr,),
        in_specs=[pl.BlockSpec((tr, width), lambda i: (i, 0))],
        out_specs=[pl.BlockSpec((tr, wd), lambda i: (i, 0)) for wd in widths],
        out_shape=[jax.ShapeDtypeStruct((depth * d_in, wd), BF16) for wd in widths],
        compiler_params=pltpu.CompilerParams(dimension_semantics=("arbitrary",), vmem_limit_bytes=VMEM_LIMIT),
        name="regroup",
    )(w_in.astype(BF16).reshape(depth * d_in, width))
    return [o.reshape(depth, d_in, wd) for o, wd in zip(outs, widths)]


def kernel(x, attn_norm_w, w_in, ret_norm_w, ret_w_o, ssd_conv_w, ssd_conv_b, ssd_dt_bias, ssd_a_log, ssd_d,
           ssd_norm_w, ssd_w_o, gla_gate_w, gla_gate_b, gla_norm_w, gla_w_o, merge_gate_b, w_out, mlp_norm_w,
           w_up, w_down, final_norm_w):
    depth = w_in.shape[0]
    seq = x.shape[1]
    d = x.shape[2]
    tl = min(MIXER_TILE, seq // 2)
    tm = min(MLP_TILE, seq)
    inv_freq = ROPE_BASE ** (-jnp.arange(0, RET_DK, 2, dtype=F32) / RET_DK)
    ang = jnp.arange(seq, dtype=F32)[:, None] * inv_freq[None, :]
    cos = jnp.cos(ang)
    sin = jnp.sin(ang)
    cos2 = jnp.concatenate([cos, cos], axis=1)
    sin2 = jnp.concatenate([-sin, sin], axis=1)

    o_ret = 0
    o_sz = 4 * RET_HEADS * RET_DK
    o_sdt = o_sz + SSD_INNER + SSD_CONV_DIM
    o_gla = o_sdt + SSD_HEADS
    o_glr = o_gla + 2 * GLA_HEADS * GLA_DK + 2 * GLA_HEADS * GLA_DV
    o_mg = o_glr + GLA_RANK
    row = lambda v: v.reshape(1, -1)
    w_r, w_s, w_g = _regroup_call(w_in, (o_ret, o_sz, o_sdt, o_gla, o_glr, o_mg, d))
    ret_wo, ssd_wo, gla_wo = ret_w_o.astype(BF16), ssd_w_o.astype(BF16), gla_w_o.astype(BF16)
    w_out_b, w_up_b, w_down_b = w_out.astype(BF16), w_up.astype(BF16), w_down.astype(BF16)
    for layer in range(depth):
        mb = [row(merge_gate_b[layer, i * d:(i + 1) * d]) for i in range(3)]
        nw = row(attn_norm_w[layer])
        ret_o = _retention_call(x, nw, w_r, cos2, sin2, row(ret_norm_w[layer]), ret_wo, mb[0], tl, layer)
        ssd_o = _ssd_call(x, nw, w_s, ssd_conv_w[layer], ssd_conv_b[layer], ssd_dt_bias[layer],
                          ssd_a_log[layer], ssd_d[layer], row(ssd_norm_w[layer]), ssd_wo, mb[1], tl, layer)
        gla_o = _gla_call(x, nw, w_g, gla_gate_w[layer], gla_gate_b[layer], row(gla_norm_w[layer]),
                          gla_wo, mb[2], tl, layer)
        x = _merge_mlp_call(x, ret_o, ssd_o, gla_o, w_out_b, row(mlp_norm_w[layer]), w_up_b, w_down_b,
                            row(final_norm_w), layer == depth - 1, tm, layer)
    return x
```
